```python
import functools
import jax
import jax.numpy as jnp
from jax import lax
import numpy as np

D_MODEL = 4096
BATCH = 4
SEQ = 2048
DEPTH = 2
DEC_BATCH = 8
DEC_SEQ = 8
PAST_LEN = 16384
PAGE_SIZE = 128

HEAD_DIM = 128
ATT_DIM = D_MODEL // 2
ATT_HEADS = ATT_DIM // HEAD_DIM
RWKV_DIM = D_MODEL // 4
RWKV_HEAD_DIM = 64
RWKV_HEADS = RWKV_DIM // RWKV_HEAD_DIM
DECAY_LORA = 64
AAA_LORA = 64
GATE_LORA = 160
RWKV_PROJ = 3 * RWKV_DIM + DECAY_LORA + AAA_LORA + GATE_LORA
CONV_DIM = D_MODEL // 4
CONV_W = 3
PROJ_W = 3 * ATT_DIM + RWKV_PROJ + 3 * CONV_DIM
D_FF = ((8 * D_MODEL // 3 + 255) // 256) * 256
MOBA_BLOCK = 256
MOBA_TOPK = 3
PAGES_PER_BLOCK = MOBA_BLOCK // PAGE_SIZE
ROPE_THETA = 10000.0
RMS_EPS = 1e-6
GN_EPS = 64e-5
Q_CHUNK = 16
NEG = -1e30

kernel_name = "hybrid_moba_rwkv7_shortconv_decoder_step"


def rms_norm(x, g):
    xf = x.astype(jnp.float32)
    y = xf * lax.rsqrt(jnp.mean(xf * xf, axis=-1, keepdims=True) + RMS_EPS)
    return (y * g.astype(jnp.float32)).astype(x.dtype)


def rope(x, pos):
    half = HEAD_DIM // 2
    inv = ROPE_THETA ** (-jnp.arange(half, dtype=jnp.float32) / half)
    ang = pos.astype(jnp.float32)[:, None] * inv[None, :]
    cos = jnp.cos(ang)[None, :, None, :]
    sin = jnp.sin(ang)[None, :, None, :]
    xf = x.astype(jnp.float32)
    x1, x2 = xf[..., :half], xf[..., half:]
    return jnp.concatenate([x1 * cos - x2 * sin, x2 * cos + x1 * sin], axis=-1).astype(x.dtype)


def causal_dwconv(u, buf, w):
    t = u.shape[1]
    ext = jnp.concatenate([buf.astype(u.dtype), u], axis=1)
    y = ext[:, 0:t] * w[0]
    for j in range(1, CONV_W):
        y = y + ext[:, j:j + t] * w[j]
    return y, ext[:, -(CONV_W - 1):]


def softmax_lse(s):
    m = jnp.max(s, axis=-1, keepdims=True)
    p = jnp.exp(s - m)
    l = jnp.sum(p, axis=-1, keepdims=True)
    return p / l, (m + jnp.log(l))[..., 0]


def merge_attn(o1, l1, o2, l2):
    lse = jnp.logaddexp(l1, l2)
    return o1 * jnp.exp(l1 - lse)[..., None] + o2 * jnp.exp(l2 - lse)[..., None]


def moba_prompt(q, k, v):
    b, s, h, d = q.shape
    f32 = jnp.float32
    scale = HEAD_DIM ** -0.5
    nb = -(-s // MOBA_BLOCK)
    sp = nb * MOBA_BLOCK
    pad = ((0, 0), (0, sp - s), (0, 0), (0, 0))
    qh = jnp.pad(q.astype(f32), pad).transpose(0, 2, 1, 3)
    kh = jnp.pad(k.astype(f32), pad).transpose(0, 2, 1, 3)
    vh = jnp.pad(v.astype(f32), pad).transpose(0, 2, 1, 3)
    qb = qh.reshape(b, h, nb, MOBA_BLOCK, d)
    kb = kh.reshape(b, h, nb, MOBA_BLOCK, d)
    vb = vh.reshape(b, h, nb, MOBA_BLOCK, d)
    causal = jnp.tril(jnp.ones((MOBA_BLOCK, MOBA_BLOCK), bool))
    s_own = jnp.einsum('bhnqd,bhnkd->bhnqk', qb, kb) * scale
    p_own, lse_own = softmax_lse(jnp.where(causal, s_own, NEG))
    o_own = jnp.einsum('bhnqk,bhnkd->bhnqd', p_own, vb).reshape(b, h, sp, d)[:, :, :s]
    lse_own = lse_own.reshape(b, h, sp)[:, :, :s]
    n_sel = min(MOBA_TOPK, nb - 1)
    if n_sel > 0:
        qblk = jnp.arange(s) // MOBA_BLOCK
        means = jnp.mean(kb, axis=3)
        gate = jnp.einsum('bhsd,bhnd->bhsn', qh[:, :, :s], means)
        gate = jnp.where(jnp.arange(nb)[None, :] < qblk[:, None], gate, NEG)
        _, idx = lax.top_k(gate, n_sel)
        valid = jnp.arange(n_sel)[None, :] < qblk[:, None]
        nc = s // Q_CHUNK
        qc = qh[:, :, :s].reshape(b, h, nc, Q_CHUNK, d).transpose(2, 0, 1, 3, 4)
        ic = idx.reshape(b, h, nc, Q_CHUNK, n_sel).transpose(2, 0, 1, 3, 4)
        mc = valid.reshape(nc, Q_CHUNK, n_sel)
        bi = jnp.arange(b)[:, None, None, None]
        hi = jnp.arange(h)[None, :, None, None]

        def chunk(args):
            qx, ix, mx = args
            kg = kb[bi, hi, ix]
            vg = vb[bi, hi, ix]
            sc = jnp.einsum('bhqd,bhqnkd->bhqnk', qx, kg) * scale
            sc = jnp.where(mx[None, None, :, :, None], sc, NEG).reshape(b, h, Q_CHUNK, -1)
            p, lse = softmax_lse(sc)
            o = jnp.einsum('bhqk,bhqkd->bhqd', p, vg.reshape(b, h, Q_CHUNK, -1, d))
            return o, lse

        o_sel, lse_sel = lax.map(chunk, (qc, ic, mc))
        o_sel = o_sel.transpose(1, 2, 0, 3, 4).reshape(b, h, s, d)
        lse_sel = lse_sel.transpose(1, 2, 0, 3).reshape(b, h, s)
        o = merge_attn(o_own, lse_own, o_sel, lse_sel)
    else:
        o = o_own
    return o.transpose(0, 2, 1, 3).reshape(b, s, h * d).astype(q.dtype)


def moba_sample(q, k, v, pool_k, pool_v, page_table):
    b, t, h, d = q.shape
    f32 = jnp.float32
    scale = HEAD_DIM ** -0.5
    qh = q.astype(f32).transpose(0, 2, 1, 3)
    kh = k.astype(f32).transpose(0, 2, 1, 3)
    vh = v.astype(f32).transpose(0, 2, 1, 3)
    n_full = PAST_LEN // MOBA_BLOCK
    own_start = n_full * MOBA_BLOCK
    mask = jnp.tril(jnp.ones((t, t), bool))
    k_own, v_own = kh, vh
    if PAST_LEN > own_start:
        n_tail = PAST_LEN - own_start
        tail_pages = page_table[:, own_start // PAGE_SIZE: PAST_LEN // PAGE_SIZE]
        kt = pool_k[tail_pages].transpose(0, 2, 1, 3, 4).reshape(b, h, n_tail, d).astype(f32)
        vt = pool_v[tail_pages].transpose(0, 2, 1, 3, 4).reshape(b, h, n_tail, d).astype(f32)
        k_own = jnp.concatenate([kt, kh], axis=2)
        v_own = jnp.concatenate([vt, vh], axis=2)
        mask = jnp.concatenate([jnp.ones((t, n_tail), bool), mask], axis=1)
    s_own = jnp.einsum('bhqd,bhkd->bhqk', qh, k_own) * scale
    p_own, lse_own = softmax_lse(jnp.where(mask, s_own, NEG))
    o_own = jnp.einsum('bhqk,bhkd->bhqd', p_own, v_own)
    n_sel = min(MOBA_TOPK, n_full)
    if n_sel > 0:
        past = pool_k[page_table[:, :n_full * PAGES_PER_BLOCK]]
        means = jnp.mean(past.reshape(b, n_full, PAGES_PER_BLOCK, h, PAGE_SIZE, d), axis=(2, 4), dtype=f32)
        gate = jnp.einsum('bhqd,bnhd->bhqn', qh, means)
        _, idx = lax.top_k(gate, n_sel)
        logical = idx[..., None] * PAGES_PER_BLOCK + jnp.arange(PAGES_PER_BLOCK)
        phys = page_table[jnp.arange(b)[:, None, None, None, None], logical]
        hi = jnp.arange(h)[None, :, None, None, None]
        kg = pool_k[phys, hi].astype(f32).reshape(b, h, t, -1, d)
        vg = pool_v[phys, hi].astype(f32).reshape(b, h, t, -1, d)
        s_sel = jnp.einsum('bhqd,bhqkd->bhqk', qh, kg) * scale
        p_sel, lse_sel = softmax_lse(s_sel)
        o_sel = jnp.einsum('bhqk,bhqkd->bhqd', p_sel, vg)
        o = merge_attn(o_own, lse_own, o_sel, lse_sel)
    else:
        o = o_own
    return o.transpose(0, 2, 1, 3).reshape(b, t, h * d).astype(q.dtype)


def rwkv7_time_mix(p, shift_prev, wkv0, mu, w0, w_up, a0, a_up, g_up, k_k, k_a, r_k, ln_w, ln_b):
    b, t, _ = p.shape
    f32 = jnp.float32
    pf = p.astype(f32)
    prev = jnp.concatenate([shift_prev[:, None, :].astype(f32), pf[:, :-1]], axis=1)
    pm = pf + (prev - pf) * mu.astype(f32)
    r, k, v, wl, al, gl = jnp.split(pm, [RWKV_DIM, 2 * RWKV_DIM, 3 * RWKV_DIM, 3 * RWKV_DIM + DECAY_LORA,
                                         3 * RWKV_DIM + DECAY_LORA + AAA_LORA], axis=-1)
    w = -jax.nn.softplus(-(w0.astype(f32) + jnp.tanh(wl) @ w_up.astype(f32))) - 0.5
    decay = jnp.exp(-jnp.exp(w))
    a = jax.nn.sigmoid(a0.astype(f32) + al @ a_up.astype(f32))
    g = jax.nn.sigmoid(gl) @ g_up.astype(f32)

    def heads(z):
        return z.reshape(b, t, RWKV_HEADS, RWKV_HEAD_DIM)

    kk = heads(k * k_k.astype(f32))
    kk = kk / jnp.maximum(jnp.sqrt(jnp.sum(kk * kk, axis=-1, keepdims=True)), 1e-12)
    k = k * (1.0 + (a - 1.0) * k_a.astype(f32))
    r4, k4, v4, a4, d4 = heads(r), heads(k), heads(v), heads(a), heads(decay)

    def step(S, xs):
        r_t, k_t, v_t, kk_t, a_t, d_t = xs
        sa = jnp.einsum('bhvk,bhk->bhv', S, -kk_t)
        S = (S * d_t[:, :, None, :] + sa[..., None] * (kk_t * a_t)[:, :, None, :]
             + v_t[..., None] * k_t[:, :, None, :])
        return S, jnp.einsum('bhvk,bhk->bhv', S, r_t)

    tm = lambda z: jnp.moveaxis(z, 1, 0)
    s_last, o = lax.scan(step, wkv0.astype(f32), (tm(r4), tm(k4), tm(v4), tm(kk), tm(a4), tm(d4)))
    o = jnp.moveaxis(o, 0, 1)
    mean = jnp.mean(o, axis=-1, keepdims=True)
    var = jnp.mean(jnp.square(o - mean), axis=-1, keepdims=True)
    o = ((o - mean) * lax.rsqrt(var + GN_EPS)).reshape(b, t, RWKV_DIM)
    o = o * ln_w.astype(f32) + ln_b.astype(f32)
    bonus = jnp.sum(r4 * k4 * r_k.astype(f32), axis=-1, keepdims=True) * v4
    out = (o + bonus.reshape(b, t, RWKV_DIM)) * g
    return out.astype(p.dtype), s_last.astype(wkv0.dtype), p[:, -1]


def trunk_layer(x, c, pos, attn_fn, shift0, wkv0, conv0, ffn0, lp):
    (w_ada, b_ada, g_pre_mix, g_post_mix, g_pre_ffn, g_post_ffn, w_in,
     rw_params, conv_w, w_o, w_ffn_in, ffn_conv_w, w_ffn_out) = lp
    b, t, _ = x.shape
    ada = jax.nn.silu(c) @ w_ada + b_ada
    sh_m, sc_m, gt_m, sh_f, sc_f, gt_f = [z[:, None, :] for z in jnp.split(ada, 6, axis=-1)]
    h = rms_norm(x, g_pre_mix) * (1 + sc_m) + sh_m
    proj = h @ w_in
    q, k, v, p_rw, p_cv = jnp.split(proj, [ATT_DIM, 2 * ATT_DIM, 3 * ATT_DIM, 3 * ATT_DIM + RWKV_PROJ], axis=-1)
    q = rope(q.reshape(b, t, ATT_HEADS, HEAD_DIM), pos)
    k = rope(k.reshape(b, t, ATT_HEADS, HEAD_DIM), pos)
    v = v.reshape(b, t, ATT_HEADS, HEAD_DIM)
    y_att = attn_fn(q, k, v)
    y_rw, wkv1, shift1 = rwkv7_time_mix(p_rw, shift0, wkv0, *rw_params)
    gb, gc, hc = jnp.split(p_cv, 3, axis=-1)
    u, conv1 = causal_dwconv(gc * hc, conv0, conv_w)
    y_cv = gb * u
    mix = jnp.concatenate([y_att, y_rw, y_cv], axis=-1) @ w_o
    x = x + (1 + gt_m) * rms_norm(mix, g_post_mix)
    h = rms_norm(x, g_pre_ffn) * (1 + sc_f) + sh_f
    gate, up = jnp.split(h @ w_ffn_in, 2, axis=-1)
    gate, ffn1 = causal_dwconv(gate, ffn0, ffn_conv_w)
    f = (jax.nn.silu(gate) * up) @ w_ffn_out
    x = x + (1 + gt_f) * rms_norm(f, g_post_ffn)
    return x, (k, v, wkv1, shift1, conv1, ffn1)


def setup_inputs(seed: int = 0) -> dict:
    key = jax.random.key(seed)
    ks = iter(jax.random.split(key, 48))
    f32 = jnp.float32

    def nrm(shape, s):
        return jax.random.normal(next(ks), shape, f32) * s

    def uni(shape, lo, hi):
        return jax.random.uniform(next(ks), shape, f32, lo, hi)

    n_pages = PAST_LEN // PAGE_SIZE
    n_pool = (DEC_BATCH * n_pages * 5) // 4
    page_table = jax.random.permutation(next(ks), n_pool)[: DEC_BATCH * n_pages]
    page_table = page_table.reshape(DEC_BATCH, n_pages).astype(jnp.int32)
    dm = D_MODEL ** -0.5
    return {
        "x_prompt": nrm((BATCH, SEQ, D_MODEL), 1.0),
        "x_sample": nrm((DEC_BATCH, DEC_SEQ, D_MODEL), 1.0),
        "c_prompt": nrm((BATCH, D_MODEL), 1.0),
        "c_sample": nrm((DEC_BATCH, D_MODEL), 1.0),
        "cache_k": nrm((DEPTH, n_pool, ATT_HEADS, PAGE_SIZE, HEAD_DIM), 1.0),
        "cache_v": nrm((DEPTH, n_pool, ATT_HEADS, PAGE_SIZE, HEAD_DIM), 1.0),
        "state_wkv": nrm((DEPTH, DEC_BATCH, RWKV_HEADS, RWKV_HEAD_DIM, RWKV_HEAD_DIM), 0.1),
        "state_shift": nrm((DEPTH, DEC_BATCH, RWKV_PROJ), 1.0),
        "state_conv": nrm((DEPTH, DEC_BATCH, CONV_W - 1, CONV_DIM), 1.0),
        "state_ffn": nrm((DEPTH, DEC_BATCH, CONV_W - 1, D_FF), 1.0),
        "page_table": page_table,
        "w_ada": nrm((DEPTH, D_MODEL, 6 * D_MODEL), 0.2 * dm),
        "b_ada": nrm((DEPTH, 6 * D_MODEL), 0.01),
        "g_pre_mix": 1.0 + nrm((DEPTH, D_MODEL), 0.05),
        "g_post_mix": 1.0 + nrm((DEPTH, D_MODEL), 0.05),
        "g_pre_ffn": 1.0 + nrm((DEPTH, D_MODEL), 0.05),
        "g_post_ffn": 1.0 + nrm((DEPTH, D_MODEL), 0.05),
        "w_in": nrm((DEPTH, D_MODEL, PROJ_W), dm),
        "rw_mu": uni((DEPTH, RWKV_PROJ), 0.0, 1.0),
        "rw_w0": uni((DEPTH, RWKV_DIM), -6.0, -1.0),
        "rw_w_up": nrm((DEPTH, DECAY_LORA, RWKV_DIM), 0.3 * DECAY_LORA ** -0.5),
        "rw_a0": nrm((DEPTH, RWKV_DIM), 0.5),
        "rw_a_up": nrm((DEPTH, AAA_LORA, RWKV_DIM), 0.3 * AAA_LORA ** -0.5),
        "rw_g_up": nrm((DEPTH, GATE_LORA, RWKV_DIM), GATE_LORA ** -0.5),
        "rw_k_k": 0.85 + nrm((DEPTH, RWKV_DIM), 0.1),
        "rw_k_a": 1.0 + nrm((DEPTH, RWKV_DIM), 0.1),
        "rw_r_k": nrm((DEPTH, RWKV_HEADS, RWKV_HEAD_DIM), 0.1),
        "rw_ln_w": 1.0 + nrm((DEPTH, RWKV_DIM), 0.05),
        "rw_ln_b": nrm((DEPTH, RWKV_DIM), 0.01),
        "conv_w": nrm((DEPTH, CONV_W, CONV_DIM), CONV_W ** -0.5),
        "w_o": nrm((DEPTH, D_MODEL, D_MODEL), dm),
        "w_ffn_in": nrm((DEPTH, D_MODEL, 2 * D_FF), dm),
        "ffn_conv_w": nrm((DEPTH, CONV_W, D_FF), CONV_W ** -0.5),
        "w_ffn_out": nrm((DEPTH, D_FF, D_MODEL), D_FF ** -0.5),
    }


def reference(x_prompt, x_sample, c_prompt, c_sample, cache_k, cache_v, state_wkv, state_shift,
              state_conv, state_ffn, page_table, w_ada, b_ada, g_pre_mix, g_post_mix, g_pre_ffn,
              g_post_ffn, w_in, rw_mu, rw_w0, rw_w_up, rw_a0, rw_a_up, rw_g_up, rw_k_k, rw_k_a,
              rw_r_k, rw_ln_w, rw_ln_b, conv_w, w_o, w_ffn_in, ffn_conv_w, w_ffn_out):
    dt = x_prompt.dtype
    pos_p = jnp.arange(SEQ, dtype=jnp.int32)
    pos_s = PAST_LEN + jnp.arange(DEC_SEQ, dtype=jnp.int32)
    shift0 = jnp.zeros((BATCH, RWKV_PROJ), dt)
    wkv0 = jnp.zeros((BATCH, RWKV_HEADS, RWKV_HEAD_DIM, RWKV_HEAD_DIM), dt)
    conv0 = jnp.zeros((BATCH, CONV_W - 1, CONV_DIM), dt)
    ffn0 = jnp.zeros((BATCH, CONV_W - 1, D_FF), dt)

    def to_pages(z):
        bb, tt, hh, dd = z.shape
        return z.reshape(bb, tt // PAGE_SIZE, PAGE_SIZE, hh, dd).transpose(0, 1, 3, 2, 4)

    xp, xs = x_prompt, x_sample
    sp, ss = [], []
    for l in range(DEPTH):
        rw = (rw_mu[l], rw_w0[l], rw_w_up[l], rw_a0[l], rw_a_up[l], rw_g_up[l], rw_k_k[l], rw_k_a[l],
              rw_r_k[l], rw_ln_w[l], rw_ln_b[l])
        lp = (w_ada[l], b_ada[l], g_pre_mix[l], g_post_mix[l], g_pre_ffn[l], g_post_ffn[l], w_in[l],
              rw, conv_w[l], w_o[l], w_ffn_in[l], ffn_conv_w[l], w_ffn_out[l])
        xp, st_p = trunk_layer(xp, c_prompt, pos_p, moba_prompt, shift0, wkv0, conv0, ffn0, lp)
        sp.append(st_p)
        attn_s = functools.partial(moba_sample, pool_k=cache_k[l], pool_v=cache_v[l], page_table=page_table)
        xs, st_s = trunk_layer(xs, c_sample, pos_s, attn_s, state_shift[l], state_wkv[l],
                               state_conv[l], state_ffn[l], lp)
        ss.append(st_s)

    k_prompt = jnp.stack([to_pages(st[0]) for st in sp])
    v_prompt = jnp.stack([to_pages(st[1]) for st in sp])
    k_sample = jnp.stack([st[0].transpose(0, 2, 1, 3) for st in ss])
    v_sample = jnp.stack([st[1].transpose(0, 2, 1, 3) for st in ss])
    wkv_prompt = jnp.stack([st[2] for st in sp])
    wkv_sample = jnp.stack([st[2] for st in ss])
    shift_prompt = jnp.stack([st[3] for st in sp])
    shift_sample = jnp.stack([st[3] for st in ss])
    conv_prompt = jnp.stack([st[4] for st in sp])
    conv_sample = jnp.stack([st[4] for st in ss])
    ffn_prompt = jnp.stack([st[5] for st in sp])
    ffn_sample = jnp.stack([st[5] for st in ss])
    return (xp, xs, k_prompt, v_prompt, k_sample, v_sample, wkv_prompt, wkv_sample,
            shift_prompt, shift_sample, conv_prompt, conv_sample, ffn_prompt, ffn_sample)
```

```python
import functools
import math

import jax
import jax.numpy as jnp
from jax import lax
from jax.experimental import pallas as pl
from jax.experimental.pallas import tpu as pltpu

F32 = jnp.float32
BF16 = jnp.bfloat16

LANES = 128
SUBLANES = 8
VMEM_LIMIT_BYTES = 56 * 1024 * 1024

HEAD_DIM = 128
RWKV_HEAD_DIM = 64
PAGE_SIZE = 128
MOBA_BLOCK = 256
MOBA_TOPK = 3
PAGES_PER_BLOCK = MOBA_BLOCK // PAGE_SIZE
ROPE_THETA = 10000.0
RMS_EPS = 1e-6
GN_EPS = 64e-5
NEG = -1e30
ATT_SCALE = HEAD_DIM ** -0.5

_NT = (((1,), (1,)), ((), ()))


def _cparams(*sem):
    return pltpu.CompilerParams(dimension_semantics=sem, vmem_limit_bytes=VMEM_LIMIT_BYTES)


def _round_up(x, m):
    return (x + m - 1) // m * m


def _pick_tile(n, pref):
    if n <= pref:
        return n
    t = pref
    while t >= LANES:
        if n % t == 0:
            return t
        t -= LANES
    return n


def _sigmoid(x):
    return 1.0 / (1.0 + jnp.exp(-x))


def _split_bf16(x):
    hi = x.astype(BF16)
    lo = (x - hi.astype(F32)).astype(BF16)
    return hi, lo


def _dot3_nt(a, b):
    ah, al = _split_bf16(a)
    bh, bl = _split_bf16(b)
    dg = functools.partial(lax.dot_general, dimension_numbers=_NT, preferred_element_type=F32)
    return dg(ah, bh) + dg(ah, bl) + dg(al, bh)


def _top3_lowest_index(gate, cand, lane):
    sel = jnp.zeros(gate.shape, F32)
    lane = lane.astype(F32)
    firsts = []
    for _ in range(MOBA_TOPK):
        gm = jnp.where(cand > 0.5, gate, -jnp.inf)
        mx = jnp.max(gm, axis=1, keepdims=True)
        is_max = jnp.where(cand > 0.5, jnp.where(gm == mx, 1.0, 0.0), 0.0)
        first = jnp.min(jnp.where(is_max > 0.5, lane, float(LANES)), axis=1, keepdims=True)
        pick = jnp.where(lane == first, 1.0, 0.0)
        sel = sel + pick
        cand = cand - pick
        firsts.append(first)
    return sel, firsts


def _ada_body(c_ref, w_ref, b_ref, o_ref):
    c = c_ref[...]
    s = c * _sigmoid(c)
    o_ref[...] = jnp.dot(s.astype(BF16), w_ref[...].astype(BF16),
                         preferred_element_type=F32) + b_ref[...]


def _ada(c_pad, w, b):
    d, n = w.shape
    tn = _pick_tile(n, 512)
    return pl.pallas_call(
        _ada_body,
        grid=(n // tn,),
        in_specs=[pl.BlockSpec(c_pad.shape, lambda j: (0, 0)),
                  pl.BlockSpec((d, tn), lambda j: (0, j)),
                  pl.BlockSpec((1, tn), lambda j: (0, j))],
        out_specs=pl.BlockSpec((c_pad.shape[0], tn), lambda j: (0, j)),
        out_shape=jax.ShapeDtypeStruct((c_pad.shape[0], n), F32),
        compiler_params=_cparams("parallel"),
        name="ada",
    )(c_pad, w, b.reshape(1, n))


def _rms(x, g):
    ms = jnp.mean(x * x, axis=-1, keepdims=True)
    return x * lax.rsqrt(ms + RMS_EPS) * g


def _norm_mod_body(x_ref, g_ref, sc_ref, sh_ref, h_ref):
    h_ref[...] = (_rms(x_ref[...], g_ref[...]) * sc_ref[...] + sh_ref[...]).astype(h_ref.dtype)


def _mod_spec(mod, tiles_per_mod):
    return pl.BlockSpec((None,) + mod.shape[1:], lambda i: (i // tiles_per_mod, 0, 0))


def _norm_mod(x, g, sc1p, sh, tm, tiles_per_mod):
    m, d = x.shape
    row = pl.BlockSpec((tm, d), lambda i: (i, 0))
    return pl.pallas_call(
        _norm_mod_body,
        grid=(m // tm,),
        in_specs=[row, pl.BlockSpec((1, d), lambda i: (0, 0)),
                  _mod_spec(sc1p, tiles_per_mod), _mod_spec(sh, tiles_per_mod)],
        out_specs=row,
        out_shape=jax.ShapeDtypeStruct((m, d), BF16),
        compiler_params=_cparams("parallel"),
        name="norm_mod",
    )(x, g.reshape(1, d), sc1p, sh)


def _resid_body(*refs, with_h):
    if with_h:
        y_ref, x_ref, gp_ref, gate_ref, gn_ref, sc_ref, sh_ref, xo_ref, h_ref = refs
    else:
        y_ref, x_ref, gp_ref, gate_ref, xo_ref = refs
    xn = x_ref[...] + gate_ref[...] * _rms(y_ref[...], gp_ref[...])
    xo_ref[...] = xn
    if with_h:
        h_ref[...] = (_rms(xn, gn_ref[...]) * sc_ref[...] + sh_ref[...]).astype(h_ref.dtype)


def _resid(y, x, g_post, gate1p, tm, tiles_per_mod, nxt=None):
    m, d = x.shape
    row = pl.BlockSpec((tm, d), lambda i: (i, 0))
    vec = pl.BlockSpec((1, d), lambda i: (0, 0))
    ins = [y, x, g_post.reshape(1, d), gate1p]
    specs = [row, row, vec, _mod_spec(gate1p, tiles_per_mod)]
    out_shape = [jax.ShapeDtypeStruct((m, d), F32)]
    out_specs = [row]
    if nxt is not None:
        g_next, sc1p, sh = nxt
        ins += [g_next.reshape(1, d), sc1p, sh]
        specs += [vec, _mod_spec(sc1p, tiles_per_mod), _mod_spec(sh, tiles_per_mod)]
        out_shape.append(jax.ShapeDtypeStruct((m, d), BF16))
        out_specs.append(row)
    res = pl.pallas_call(
        functools.partial(_resid_body, with_h=nxt is not None),
        grid=(m // tm,),
        in_specs=specs,
        out_specs=out_specs,
        out_shape=out_shape,
        compiler_params=_cparams("parallel"),
        name="resid",
    )(*ins)
    return res if nxt is not None else (res[0], None)


def _mm_body(*refs, k_sizes):
    n_lhs = len(k_sizes)
    w_ref, o_ref = refs[n_lhs], refs[n_lhs + 1]
    acc = None
    off = 0
    for a_ref, ks in zip(refs[:n_lhs], k_sizes):
        part = jnp.dot(a_ref[...], w_ref[off:off + ks, :], preferred_element_type=F32)
        acc = part if acc is None else acc + part
        off += ks
    o_ref[...] = acc.astype(o_ref.dtype)


def _matmul(lhs_list, w, tm, tn):
    m = lhs_list[0].shape[0]
    k_sizes = tuple(a.shape[1] for a in lhs_list)
    k, n = w.shape
    assert sum(k_sizes) == k and m % tm == 0 and n % tn == 0
    return pl.pallas_call(
        functools.partial(_mm_body, k_sizes=k_sizes),
        grid=(n // tn, m // tm),
        in_specs=[pl.BlockSpec((tm, ks), lambda j, i: (i, 0)) for ks in k_sizes]
        + [pl.BlockSpec((k, tn), lambda j, i: (0, j))],
        out_specs=pl.BlockSpec((tm, tn), lambda j, i: (i, j)),
        out_shape=jax.ShapeDtypeStruct((m, n), F32),
        compiler_params=_cparams("parallel", "parallel"),
        name="matmul",
    )(*lhs_list, w)


def _rope_tables(pos):
    half = HEAD_DIM // 2
    inv = ROPE_THETA ** (-jnp.arange(half, dtype=F32) / half)
    ang = pos.astype(F32)[:, None] * inv[None, :]
    cos, sin = jnp.cos(ang), jnp.sin(ang)
    return jnp.concatenate([cos, cos], axis=1), jnp.concatenate([-sin, sin], axis=1)


def _rope(x, cos2, sin2):
    return x * cos2 + pltpu.roll(x, HEAD_DIM // 2, 1) * sin2


def _moba_p_body(q_ref, k_ref, v_ref, cosq_ref, sinq_ref, cosk_ref, sink_ref, e_ref,
                 y_ref, ko_ref, vo_ref, kbf, vbf, means):
    n = pl.program_id(2)
    s_len = k_ref.shape[0]
    nb = s_len // MOBA_BLOCK
    n_pages = s_len // PAGE_SIZE

    @pl.when(n == 0)
    def _():
        kr = _rope(k_ref[...], cosk_ref[...], sink_ref[...])
        v = v_ref[...]
        ko_ref[...] = kr.reshape(n_pages, PAGE_SIZE, HEAD_DIM)
        vo_ref[...] = v.reshape(n_pages, PAGE_SIZE, HEAD_DIM)
        kbf[...] = kr.astype(BF16)
        vbf[...] = v.astype(BF16)
        means[...] = jnp.zeros(means.shape, F32)
        means[0:nb, :] = jnp.mean(kr.reshape(nb, MOBA_BLOCK, HEAD_DIM), axis=1)

    qr = _rope(q_ref[...], cosq_ref[...], sinq_ref[...])
    gate = _dot3_nt(qr, means[...])
    lane = lax.broadcasted_iota(jnp.int32, gate.shape, 1)
    cand = jnp.where(lane < n, 1.0, 0.0)
    sel, _ = _top3_lowest_index(gate, cand, lane)
    picked = jnp.dot(sel.astype(BF16), e_ref[...], preferred_element_type=F32)

    s = lax.dot_general(qr.astype(BF16), kbf[...], _NT, preferred_element_type=F32) * ATT_SCALE
    col = lax.broadcasted_iota(jnp.int32, s.shape, 1)
    row = lax.broadcasted_iota(jnp.int32, s.shape, 0)
    own = jnp.where(col >= n * MOBA_BLOCK, jnp.where(col <= n * MOBA_BLOCK + row, 1.0, 0.0), 0.0)
    s = jnp.where(picked + own > 0.5, s, NEG)
    mx = jnp.max(s, axis=1, keepdims=True)
    p = jnp.exp(s - mx)
    l = jnp.sum(p, axis=1, keepdims=True)
    o = jnp.dot(p.astype(BF16), vbf[...], preferred_element_type=F32) / l
    y_ref[...] = o.astype(y_ref.dtype)


def _moba_prompt(p_qkv, nseq, s_len, n_heads, cos2, sin2):
    nb = s_len // MOBA_BLOCK
    n_pages = s_len // PAGE_SIZE
    m = nseq * s_len
    e = (jnp.arange(s_len)[None, :] // MOBA_BLOCK == jnp.arange(LANES)[:, None]).astype(BF16)
    blk = pl.BlockSpec((MOBA_BLOCK, HEAD_DIM), lambda b, h, n: (b * nb + n, h))
    full = lambda off: pl.BlockSpec((s_len, HEAD_DIM), lambda b, h, n: (b, off + h))
    tab_q = pl.BlockSpec((MOBA_BLOCK, HEAD_DIM), lambda b, h, n: (n, 0))
    tab_k = pl.BlockSpec((s_len, HEAD_DIM), lambda b, h, n: (0, 0))
    pages = pl.BlockSpec((None, n_pages, None, PAGE_SIZE, HEAD_DIM), lambda b, h, n: (b, 0, h, 0, 0))
    page_shape = jax.ShapeDtypeStruct((nseq, n_pages, n_heads, PAGE_SIZE, HEAD_DIM), F32)
    return pl.pallas_call(
        _moba_p_body,
        grid=(nseq, n_heads, nb),
        in_specs=[blk, full(n_heads), full(2 * n_heads), tab_q, tab_q, tab_k, tab_k,
                  pl.BlockSpec((LANES, s_len), lambda b, h, n: (0, 0))],
        out_specs=[blk, pages, pages],
        out_shape=[jax.ShapeDtypeStruct((m, n_heads * HEAD_DIM), BF16), page_shape, page_shape],
        scratch_shapes=[pltpu.VMEM((s_len, HEAD_DIM), BF16), pltpu.VMEM((s_len, HEAD_DIM), BF16),
                        pltpu.VMEM((LANES, HEAD_DIM), F32)],
        compiler_params=_cparams("parallel", "parallel", "arbitrary"),
        name="moba_prompt",
    )(p_qkv, p_qkv, p_qkv, cos2, sin2, cos2, sin2, e)


def _rope_s_body(p_ref, cos_ref, sin_ref, q_o, k_o, v_o, *, n_heads):
    cos2, sin2 = cos_ref[...], sin_ref[...]
    for h in range(n_heads):
        sl = lambda part: p_ref[:, (part * n_heads + h) * HEAD_DIM:(part * n_heads + h + 1) * HEAD_DIM]
        q_o[h] = _rope(sl(0), cos2, sin2)
        k_o[h] = _rope(sl(1), cos2, sin2)
        v_o[h] = sl(2)


def _rope_sample(p_qkv, nseq, t_len, n_heads, cos2, sin2):
    shp = jax.ShapeDtypeStruct((nseq, n_heads, t_len, HEAD_DIM), F32)
    out = pl.BlockSpec((None, n_heads, t_len, HEAD_DIM), lambda b: (b, 0, 0, 0))
    tab = pl.BlockSpec((t_len, HEAD_DIM), lambda b: (0, 0))
    return pl.pallas_call(
        functools.partial(_rope_s_body, n_heads=n_heads),
        grid=(nseq,),
        in_specs=[pl.BlockSpec((t_len, p_qkv.shape[1]), lambda b: (b, 0)), tab, tab],
        out_specs=[out, out, out],
        out_shape=[shp, shp, shp],
        compiler_params=_cparams("parallel"),
        name="rope_sample",
    )(p_qkv, cos2, sin2)


def _means_body(pt_ref, k0_ref, k1_ref, o_ref, *, n_heads):
    j = pl.program_id(1)

    @pl.when(j == 0)
    def _():
        o_ref[...] = jnp.zeros(o_ref.shape, F32)

    s = (jnp.sum(k0_ref[...], axis=1) + jnp.sum(k1_ref[...], axis=1)) * (1.0 / MOBA_BLOCK)
    for h in range(n_heads):
        o_ref[h, pl.ds(j, 1), :] = s[h:h + 1, :]


def _block_means(cache_k, layer, page_table, n_full):
    nseq = page_table.shape[0]
    n_heads = cache_k.shape[2]
    assert PAGES_PER_BLOCK == 2 and n_full <= LANES
    page = lambda pg: pl.BlockSpec(
        (None, None, n_heads, PAGE_SIZE, HEAD_DIM),
        lambda b, j, pt: (layer, pt[b, PAGES_PER_BLOCK * j + pg], 0, 0, 0))
    return pl.pallas_call(
        functools.partial(_means_body, n_heads=n_heads),
        grid_spec=pltpu.PrefetchScalarGridSpec(
            num_scalar_prefetch=1,
            grid=(nseq, n_full),
            in_specs=[page(0), page(1)],
            out_specs=pl.BlockSpec((None, n_heads, LANES, HEAD_DIM), lambda b, j, pt: (b, 0, 0, 0)),
        ),
        out_shape=jax.ShapeDtypeStruct((nseq, n_heads, LANES, HEAD_DIM), F32),
        compiler_params=_cparams("parallel", "arbitrary"),
        name="block_means",
    )(page_table, cache_k, cache_k)


def _topk_s_body(q_ref, m_ref, o_ref, *, n_heads, n_full):
    for h in range(n_heads):
        gate = _dot3_nt(q_ref[h], m_ref[h])
        lane = lax.broadcasted_iota(jnp.int32, gate.shape, 1)
        cand = jnp.where(lane < n_full, 1.0, 0.0)
        _, firsts = _top3_lowest_index(gate, cand, lane)
        idx = jnp.zeros(gate.shape, F32)
        for r, first in enumerate(firsts):
            idx = jnp.where(lane == r, first, idx)
        o_ref[h] = idx.astype(jnp.int32)


def _topk_sample(q_s, means, n_full):
    nseq, n_heads, t_len, _ = q_s.shape
    return pl.pallas_call(
        functools.partial(_topk_s_body, n_heads=n_heads, n_full=n_full),
        grid=(nseq,),
        in_specs=[pl.BlockSpec((None, n_heads, t_len, HEAD_DIM), lambda b: (b, 0, 0, 0)),
                  pl.BlockSpec((None, n_heads, LANES, HEAD_DIM), lambda b: (b, 0, 0, 0))],
        out_specs=pl.BlockSpec((None, n_heads, t_len, LANES), lambda b: (b, 0, 0, 0)),
        out_shape=jax.ShapeDtypeStruct((nseq, n_heads, t_len, LANES), jnp.int32),
        compiler_params=_cparams("parallel"),
        name="topk_sample",
    )(q_s, means)


_N_SEL_PAGES = MOBA_TOPK * PAGES_PER_BLOCK


def _attn_s_body(idx_ref, pt_ref, q_ref, kn_ref, vn_ref, *refs):
    k_refs, v_refs, o_ref = refs[:_N_SEL_PAGES], refs[_N_SEL_PAGES:2 * _N_SEL_PAGES], refs[-1]
    t = pl.program_id(2)
    q = q_ref[...]
    t_len = q.shape[0]
    qb = q.astype(BF16)
    s_sel = [lax.dot_general(qb, k[...].astype(BF16), _NT, preferred_element_type=F32) * ATT_SCALE
             for k in k_refs]
    rowc = lax.broadcasted_iota(jnp.int32, (t_len, 1), 0)
    kn, vn = kn_ref[...], vn_ref[...]
    s_own = [jnp.where(rowc >= j,
                       jnp.sum(q * kn[j:j + 1, :], axis=1, keepdims=True) * ATT_SCALE, NEG)
             for j in range(t_len)]
    mx = s_own[0]
    for s in s_own[1:]:
        mx = jnp.maximum(mx, s)
    for s in s_sel:
        mx = jnp.maximum(mx, jnp.max(s, axis=1, keepdims=True))
    l = jnp.zeros_like(mx)
    o = jnp.zeros((t_len, HEAD_DIM), F32)
    for s, v in zip(s_sel, v_refs):
        p = jnp.exp(s - mx)
        l = l + jnp.sum(p, axis=1, keepdims=True)
        o = o + jnp.dot(p.astype(BF16), v[...].astype(BF16), preferred_element_type=F32)
    for j, s in enumerate(s_own):
        p = jnp.exp(s - mx)
        l = l + p
        o = o + p * vn[j:j + 1, :]
    o = o / l
    row = lax.broadcasted_iota(jnp.int32, o.shape, 0)
    o_ref[pl.ds(t, 1), :] = jnp.sum(jnp.where(row == t, o, 0.0), axis=0, keepdims=True)


def _attn_sample(q_s, k_new, v_new, cache_k, cache_v, layer, idx_flat, page_table):
    nseq, n_heads, t_len, _ = q_s.shape

    def page(s, pg):
        def imap(b, h, t, idx, pt):
            blk = idx[((b * n_heads + h) * t_len + t) * MOBA_TOPK + s]
            return (layer, pt[b, blk * PAGES_PER_BLOCK + pg], h, 0, 0)
        return pl.BlockSpec((None, None, None, PAGE_SIZE, HEAD_DIM), imap)

    pages = [page(s, pg) for s in range(MOBA_TOPK) for pg in range(PAGES_PER_BLOCK)]
    new = pl.BlockSpec((None, None, t_len, HEAD_DIM), lambda b, h, t, idx, pt: (b, h, 0, 0))
    return pl.pallas_call(
        _attn_s_body,
        grid_spec=pltpu.PrefetchScalarGridSpec(
            num_scalar_prefetch=2,
            grid=(nseq, n_heads, t_len),
            in_specs=[new, new, new] + pages + pages,
            out_specs=pl.BlockSpec((t_len, HEAD_DIM), lambda b, h, t, idx, pt: (b, h)),
        ),
        out_shape=jax.ShapeDtypeStruct((nseq * t_len, n_heads * HEAD_DIM), F32),
        compiler_params=_cparams("parallel", "parallel", "arbitrary"),
        name="attn_sample",
    )(idx_flat, page_table, q_s, k_new, v_new, *([cache_k] * _N_SEL_PAGES), *([cache_v] * _N_SEL_PAGES))


def _rw_prep_body(p_ref, halo_ref, sp_ref, mu_ref, w0_ref, wup_ref, a0_ref, aup_ref, gup_ref,
                  kk_ref, ka_ref, r_o, k_o, v_o, kk_o, a_o, d_o, g_o, *, tiles_per_seq, rdim, lora_ab):
    i = pl.program_id(0)
    pf = p_ref[...]
    first = (i % tiles_per_seq) == 0
    prev0 = jnp.where(first, sp_ref[...], halo_ref[SUBLANES - 1:SUBLANES, :])
    row = lax.broadcasted_iota(jnp.int32, pf.shape, 0)
    prev = jnp.where(row == 0, prev0, pltpu.roll(pf, 1, 0))
    pm = pf + (prev - pf) * mu_ref[...]
    r, k, v = pm[:, 0:rdim], pm[:, rdim:2 * rdim], pm[:, 2 * rdim:3 * rdim]
    ca = pm[:, 3 * rdim:3 * rdim + lora_ab]
    cb = pm[:, 3 * rdim + lora_ab:]
    w_lin = jnp.dot(jnp.tanh(ca).astype(BF16), wup_ref[...], preferred_element_type=F32)
    a_lin = jnp.dot(ca.astype(BF16), aup_ref[...], preferred_element_type=F32)
    g = jnp.dot(_sigmoid(cb).astype(BF16), gup_ref[...], preferred_element_type=F32)
    y = -(w0_ref[...] + w_lin)
    w = -(jnp.maximum(y, 0.0) + jnp.log(1.0 + jnp.exp(-jnp.abs(y)))) - 0.5
    a = _sigmoid(a0_ref[...] + a_lin)
    r_o[...] = r
    k_o[...] = k * (1.0 + (a - 1.0) * ka_ref[...])
    v_o[...] = v
    kk_o[...] = k * kk_ref[...]
    a_o[...] = a
    d_o[...] = jnp.exp(-jnp.exp(w))
    g_o[...] = g


def _rw_prep(p_rw, shift_prev, mu, w0, wup, a0, aup, gup, k_k, k_a, tm, tiles_per_seq, rdim, lora_ab):
    m, wp = p_rw.shape
    row = pl.BlockSpec((tm, wp), lambda i: (i, 0))
    halo = pl.BlockSpec((SUBLANES, wp), lambda i: (jnp.maximum(i * (tm // SUBLANES) - 1, 0), 0))
    const = lambda a: pl.BlockSpec(a.shape, lambda i: (0,) * a.ndim)
    out = pl.BlockSpec((tm, rdim), lambda i: (i, 0))
    shp = jax.ShapeDtypeStruct((m, rdim), F32)
    vecs = [x.reshape(1, -1) for x in (mu, w0)]
    return pl.pallas_call(
        functools.partial(_rw_prep_body, tiles_per_seq=tiles_per_seq, rdim=rdim, lora_ab=lora_ab),
        grid=(m // tm,),
        in_specs=[row, halo, pl.BlockSpec((None, 1, wp), lambda i: (i // tiles_per_seq, 0, 0)),
                  const(vecs[0]), const(vecs[1]), const(wup), const(a0.reshape(1, -1)), const(aup),
                  const(gup), const(k_k.reshape(1, -1)), const(k_a.reshape(1, -1))],
        out_specs=[out] * 7,
        out_shape=[shp] * 7,
        compiler_params=_cparams("parallel"),
        name="rwkv_prep",
    )(p_rw, p_rw, shift_prev, vecs[0], vecs[1], wup, a0.reshape(1, -1), aup, gup,
      k_k.reshape(1, -1), k_a.reshape(1, -1))


def _rw_scan_body(r_ref, k_ref, kk_ref, a_ref, d_ref, v_ref, rk_ref, lnw_ref, lnb_ref, s0_ref,
                  o_ref, so_ref, state, orow, *, halves, chains):
    pid = pl.program_id(0)
    tc, nv = v_ref.shape[0], v_ref.shape[1]
    n_val = nv * halves

    @pl.when(pid == 0)
    def _():
        state[...] = s0_ref[...]

    def lane_group_sum(x):
        shift = chains
        for _ in range(int(math.log2(halves))):
            x = x + pltpu.roll(x, shift, 1)
            shift *= 2
        return x

    def step(t, carry):
        kk = kk_ref[t]
        n2 = jnp.sum(kk * kk, axis=0, keepdims=True)
        kkn = kk / jnp.maximum(jnp.sqrt(n2), 1e-12)
        a, d, r, kp = a_ref[t], d_ref[t], r_ref[t], k_ref[t]
        b = -kkn
        c = kkn * a
        vt = v_ref[t]
        for vi in range(nv):
            sv = state[vi]
            u = jnp.sum(sv * b, axis=0, keepdims=True)
            sn = sv * d + u * c + vt[vi:vi + 1, :] * kp
            state[vi] = sn
            orow[vi:vi + 1, :] = jnp.sum(sn * r, axis=0, keepdims=True)
        o = orow[...]
        mean = jnp.sum(lane_group_sum(o), axis=0, keepdims=True) * (1.0 / n_val)
        dev = o - mean
        var = jnp.sum(lane_group_sum(dev * dev), axis=0, keepdims=True) * (1.0 / n_val)
        on = dev * lax.rsqrt(var + GN_EPS) * lnw_ref[...] + lnb_ref[...]
        bonus = jnp.sum(r * kp * rk_ref[...], axis=0, keepdims=True) * vt
        o_ref[t] = on + bonus
        return carry

    lax.fori_loop(0, tc, step, 0)

    @pl.when(pid == pl.num_programs(0) - 1)
    def _():
        so_ref[...] = state[...]


def _rw_scan(r, k, kk, a, d, v, rk_tile, lnw_tile, lnb_tile, s0, tc, halves, chains):
    t_len, kdim, _ = r.shape
    nv = v.shape[1]
    ktile = pl.BlockSpec((tc, kdim, LANES), lambda i: (i, 0, 0))
    vtile = pl.BlockSpec((tc, nv, LANES), lambda i: (i, 0, 0))
    const = lambda x: pl.BlockSpec(x.shape, lambda i: (0,) * x.ndim)
    return pl.pallas_call(
        functools.partial(_rw_scan_body, halves=halves, chains=chains),
        grid=(t_len // tc,),
        in_specs=[ktile] * 5 + [vtile, const(rk_tile), const(lnw_tile), const(lnb_tile), const(s0)],
        out_specs=[vtile, const(s0)],
        out_shape=[jax.ShapeDtypeStruct(v.shape, F32), jax.ShapeDtypeStruct(s0.shape, F32)],
        scratch_shapes=[pltpu.VMEM(s0.shape, F32), pltpu.VMEM((nv, LANES), F32)],
        compiler_params=_cparams("arbitrary"),
        name="rwkv_scan",
    )(r, k, kk, a, d, v, rk_tile, lnw_tile, lnb_tile, s0)


def _shifted_rows(u, halo, state, first):
    p1 = jnp.where(first, state[1:2, :], halo[SUBLANES - 1:SUBLANES, :])
    p2 = jnp.where(first, state[0:1, :], halo[SUBLANES - 2:SUBLANES - 1, :])
    row = lax.broadcasted_iota(jnp.int32, u.shape, 0)
    u1 = jnp.where(row == 0, p1, pltpu.roll(u, 1, 0))
    u2 = jnp.where(row == 0, p2, jnp.where(row == 1, p1, pltpu.roll(u, 2, 0)))
    return u1, u2


def _conv_mix_body(gb_ref, gc_ref, hc_ref, gch_ref, hch_ref, c0_ref, w_ref, o_ref, g_ref,
                   ycv_ref, yrw_ref, c1_ref, *, tiles_per_seq):
    i = pl.program_id(0)
    first = (i % tiles_per_seq) == 0
    u = gc_ref[...] * hc_ref[...]
    u1, u2 = _shifted_rows(u, gch_ref[...] * hch_ref[...], c0_ref[...], first)
    w = w_ref[...]
    y = u2 * w[0:1, :] + u1 * w[1:2, :] + u * w[2:3, :]
    ycv_ref[...] = (gb_ref[...] * y).astype(ycv_ref.dtype)
    yrw_ref[...] = (o_ref[...] * g_ref[...]).astype(yrw_ref.dtype)

    @pl.when((i % tiles_per_seq) == tiles_per_seq - 1)
    def _():
        c1_ref[...] = u[u.shape[0] - 2:, :]


def _conv_mix(p_cv, conv0, conv_w, o_rw, g_rw, tm, tiles_per_seq):
    m = p_cv.shape[0]
    c = conv_w.shape[1]
    nseq = conv0.shape[0]
    col = lambda j: pl.BlockSpec((tm, c), lambda i: (i, j))
    halo = lambda j: pl.BlockSpec((SUBLANES, c), lambda i: (jnp.maximum(i * (tm // SUBLANES) - 1, 0), j))
    state = pl.BlockSpec((None, 2, c), lambda i: (i // tiles_per_seq, 0, 0))
    rw = pl.BlockSpec((tm, o_rw.shape[1]), lambda i: (i, 0))
    return pl.pallas_call(
        functools.partial(_conv_mix_body, tiles_per_seq=tiles_per_seq),
        grid=(m // tm,),
        in_specs=[col(0), col(1), col(2), halo(1), halo(2), state,
                  pl.BlockSpec((3, c), lambda i: (0, 0)), rw, rw],
        out_specs=[col(0), rw, state],
        out_shape=[jax.ShapeDtypeStruct((m, c), BF16), jax.ShapeDtypeStruct(o_rw.shape, BF16),
                   jax.ShapeDtypeStruct((nseq, 2, c), F32)],
        compiler_params=_cparams("arbitrary"),
        name="conv_mix",
    )(p_cv, p_cv, p_cv, p_cv, p_cv, conv0, conv_w, o_rw, g_rw)


def _ffn_act_body(gate_ref, up_ref, halo_ref, s0_ref, w_ref, f_ref, s1_ref, *, tiles_per_seq):
    i = pl.program_id(1)
    first = (i % tiles_per_seq) == 0
    u = gate_ref[...]
    u1, u2 = _shifted_rows(u, halo_ref[...], s0_ref[...], first)
    w = w_ref[...]
    y = u2 * w[0:1, :] + u1 * w[1:2, :] + u * w[2:3, :]
    f_ref[...] = (y * _sigmoid(y) * up_ref[...]).astype(f_ref.dtype)

    @pl.when((i % tiles_per_seq) == tiles_per_seq - 1)
    def _():
        s1_ref[...] = u[u.shape[0] - 2:, :]


def _ffn_act(gu, ffn0, ffn_w, tm, tc, tiles_per_seq):
    m = gu.shape[0]
    dff = ffn_w.shape[1]
    nseq = ffn0.shape[0]
    ncol = dff // tc
    state = pl.BlockSpec((None, 2, tc), lambda j, i: (i // tiles_per_seq, 0, j))
    return pl.pallas_call(
        functools.partial(_ffn_act_body, tiles_per_seq=tiles_per_seq),
        grid=(ncol, m // tm),
        in_specs=[pl.BlockSpec((tm, tc), lambda j, i: (i, j)),
                  pl.BlockSpec((tm, tc), lambda j, i: (i, ncol + j)),
                  pl.BlockSpec((SUBLANES, tc), lambda j, i: (jnp.maximum(i * (tm // SUBLANES) - 1, 0), j)),
                  state, pl.BlockSpec((3, tc), lambda j, i: (0, j))],
        out_specs=[pl.BlockSpec((tm, tc), lambda j, i: (i, j)), state],
        out_shape=[jax.ShapeDtypeStruct((m, dff), BF16), jax.ShapeDtypeStruct((nseq, 2, dff), F32)],
        compiler_params=_cparams("parallel", "arbitrary"),
        name="ffn_act",
    )(gu, gu, gu, ffn0, ffn_w)


def _chain_geometry(nseq, n_heads):
    chains = nseq * n_heads
    assert LANES % chains == 0
    halves = LANES // chains
    assert RWKV_HEAD_DIM % halves == 0
    return chains, halves, RWKV_HEAD_DIM // halves


def _to_chain_k(x, nseq, t_len, n_heads, halves):
    x = x.reshape(nseq, t_len, n_heads, RWKV_HEAD_DIM).transpose(1, 3, 0, 2)
    return jnp.tile(x.reshape(t_len, RWKV_HEAD_DIM, nseq * n_heads), (1, 1, halves))


def _to_chain_v(x, nseq, t_len, n_heads, halves, nv):
    x = x.reshape(nseq, t_len, n_heads, halves, nv).transpose(1, 4, 3, 0, 2)
    return x.reshape(t_len, nv, LANES)


def _from_chain_v(x, nseq, t_len, n_heads, halves, nv):
    x = x.reshape(t_len, nv, halves, nseq, n_heads).transpose(3, 0, 4, 2, 1)
    return x.reshape(nseq * t_len, n_heads * RWKV_HEAD_DIM)


def _state_to_chain(s, nseq, n_heads, halves, nv):
    s = s.reshape(nseq, n_heads, halves, nv, RWKV_HEAD_DIM).transpose(3, 4, 2, 0, 1)
    return s.reshape(nv, RWKV_HEAD_DIM, LANES)


def _state_from_chain(s, nseq, n_heads, halves, nv):
    s = s.reshape(nv, RWKV_HEAD_DIM, halves, nseq, n_heads).transpose(3, 4, 2, 0, 1)
    return s.reshape(nseq, n_heads, RWKV_HEAD_DIM, RWKV_HEAD_DIM)


def _head_tile_k(x, nseq, halves):
    return jnp.tile(x.T, (1, halves * nseq))


def _head_tile_v(x, nseq, n_heads, halves, nv):
    x = x.reshape(n_heads, halves, nv).transpose(2, 1, 0)
    return jnp.broadcast_to(x[:, :, None, :], (nv, halves, nseq, n_heads)).reshape(nv, LANES)


def _mods(ada_rows, d, per_row, rows_per_seq):
    parts = [ada_rows[:, i * d:(i + 1) * d] for i in range(6)]
    sh_m, sc_m, gt_m, sh_f, sc_f, gt_f = parts
    out = (sh_m, 1.0 + sc_m, 1.0 + gt_m, sh_f, 1.0 + sc_f, 1.0 + gt_f)
    if per_row:
        return tuple(jnp.repeat(z, rows_per_seq, axis=0)[None] for z in out)
    return tuple(z[:, None, :] for z in out)


def _layer(x, h_in, mods, nseq, t_len, lw, attn_fn, states, tiles):
    (sh_m, sc_m, gt_m, sh_f, sc_f, gt_f) = mods
    shift0, wkv0, conv0, ffn0 = states
    d = x.shape[1]
    tm_mm, tm_ew, tm_seq, tpm = tiles["mm"], tiles["ew"], tiles["seq"], tiles["tiles_per_mod"]
    tiles_per_seq = t_len // tm_seq
    rdim = lw["rdim"]
    n_rw_heads = rdim // RWKV_HEAD_DIM

    p_qkv = _matmul([h_in], lw["w_qkv"], tm_mm, _pick_tile(lw["w_qkv"].shape[1], 512))
    p_rw = _matmul([h_in], lw["w_rw"], tm_mm, _pick_tile(lw["w_rw"].shape[1], 512))
    p_cv = _matmul([h_in], lw["w_cv"], tm_mm, _pick_tile(lw["w_cv"].shape[1], 512))

    y_att, k_new, v_new = attn_fn(p_qkv)

    r, k2, v, kk, a, dec, g = _rw_prep(p_rw, shift0, lw["mu"], lw["w0"], lw["wup"], lw["a0"], lw["aup"],
                                        lw["gup"], lw["k_k"], lw["k_a"], tm_seq, tiles_per_seq, rdim,
                                        lw["lora_ab"])
    chains, halves, nv = _chain_geometry(nseq, n_rw_heads)
    ck = lambda z: _to_chain_k(z, nseq, t_len, n_rw_heads, halves)
    o_c, s_c = _rw_scan(ck(r), ck(k2), ck(kk), ck(a), ck(dec),
                        _to_chain_v(v, nseq, t_len, n_rw_heads, halves, nv),
                        _head_tile_k(lw["r_k"], nseq, halves),
                        _head_tile_v(lw["ln_w"], nseq, n_rw_heads, halves, nv),
                        _head_tile_v(lw["ln_b"], nseq, n_rw_heads, halves, nv),
                        _state_to_chain(wkv0, nseq, n_rw_heads, halves, nv),
                        tiles["scan"], halves, chains)
    o_rw = _from_chain_v(o_c, nseq, t_len, n_rw_heads, halves, nv)
    wkv1 = _state_from_chain(s_c, nseq, n_rw_heads, halves, nv)
    shift1 = p_rw.reshape(nseq, t_len, -1)[:, -1, :lw["rw_proj"]]

    y_cv, y_rw, conv1 = _conv_mix(p_cv, conv0, lw["conv_w"], o_rw, g, tm_seq, tiles_per_seq)

    mix = _matmul([y_att, y_rw, y_cv], lw["w_o"], tm_mm, _pick_tile(d, 512))
    x1, h2 = _resid(mix, x, lw["g_post_mix"], gt_m, tm_ew, tpm, nxt=(lw["g_pre_ffn"], sc_f, sh_f))

    gu = _matmul([h2], lw["w_ffn_in"], tm_mm, _pick_tile(lw["w_ffn_in"].shape[1], 512))
    dff = lw["w_ffn_out"].shape[0]
    f, ffn1 = _ffn_act(gu, ffn0, lw["ffn_conv_w"], tm_seq, _pick_tile(dff, 256), tiles_per_seq)
    fo = _matmul([f], lw["w_ffn_out"], tm_mm, _pick_tile(d, 256))
    return x1, fo, gt_f, (k_new, v_new, wkv1, shift1, conv1, ffn1)


def kernel(x_prompt, x_sample, c_prompt, c_sample, cache_k, cache_v, state_wkv, state_shift, state_conv, state_ffn, page_table, w_ada, b_ada, g_pre_mix, g_post_mix, g_pre_ffn, g_post_ffn, w_in, rw_mu, rw_w0, rw_w_up, rw_a0, rw_a_up, rw_g_up, rw_k_k, rw_k_a, rw_r_k, rw_ln_w, rw_ln_b, conv_w, w_o, w_ffn_in, ffn_conv_w, w_ffn_out):
    nb_p, s_len, d = x_prompt.shape
    nb_s, t_s, _ = x_sample.shape
    depth = w_in.shape[0]
    att = cache_k.shape[2] * HEAD_DIM
    n_heads = att // HEAD_DIM
    rdim = rw_w0.shape[1]
    rw_proj = rw_mu.shape[1]
    lora_ab = rw_w_up.shape[1] + rw_a_up.shape[1]
    lora_g = rw_g_up.shape[1]
    lora_g_pad = _round_up(lora_g, LANES)
    rw_pad = _round_up(3 * rdim + lora_ab + lora_g_pad, 512)
    lora_g_pad = rw_pad - 3 * rdim - lora_ab
    assert lora_ab == LANES
    past_len = page_table.shape[1] * PAGE_SIZE
    assert past_len % MOBA_BLOCK == 0 and s_len % MOBA_BLOCK == 0
    n_full = past_len // MOBA_BLOCK
    conv_dim = conv_w.shape[2]
    dff = ffn_conv_w.shape[2]

    xp = x_prompt.reshape(nb_p * s_len, d)
    xs = x_sample.reshape(nb_s * t_s, d)
    n_c = nb_p + nb_s
    c_all = jnp.concatenate([c_prompt, c_sample, jnp.zeros((_round_up(n_c, SUBLANES) - n_c, d), F32)], 0)

    cos_p, sin_p = _rope_tables(jnp.arange(s_len, dtype=jnp.int32))
    cos_s, sin_s = _rope_tables(past_len + jnp.arange(t_s, dtype=jnp.int32))

    tiles_p = dict(mm=_pick_tile(nb_p * s_len, 512), ew=MOBA_BLOCK, seq=MOBA_BLOCK, scan=16,
                   tiles_per_mod=s_len // MOBA_BLOCK)
    tiles_s = dict(mm=nb_s * t_s, ew=nb_s * t_s, seq=t_s, scan=t_s, tiles_per_mod=1)

    zeros_p = (jnp.zeros((nb_p, 1, rw_pad), F32),
               jnp.zeros((nb_p, rdim // RWKV_HEAD_DIM, RWKV_HEAD_DIM, RWKV_HEAD_DIM), F32),
               jnp.zeros((nb_p, 2, conv_dim), F32), jnp.zeros((nb_p, 2, dff), F32))

    def layer_weights(l):
        wi = w_in[l]
        pad_rows = lambda w, before, total: jnp.pad(w, ((before, total - before - w.shape[0]), (0, 0)))
        return dict(
            w_qkv=wi[:, :3 * att].astype(BF16),
            w_rw=jnp.pad(wi[:, 3 * att:3 * att + rw_proj].astype(BF16), ((0, 0), (0, rw_pad - rw_proj))),
            w_cv=wi[:, 3 * att + rw_proj:].astype(BF16),
            w_o=w_o[l].astype(BF16), w_ffn_in=w_ffn_in[l].astype(BF16), w_ffn_out=w_ffn_out[l].astype(BF16),
            mu=jnp.pad(rw_mu[l], (0, rw_pad - rw_proj)), w0=rw_w0[l], a0=rw_a0[l],
            wup=pad_rows(rw_w_up[l], 0, lora_ab).astype(BF16),
            aup=pad_rows(rw_a_up[l], rw_w_up.shape[1], lora_ab).astype(BF16),
            gup=pad_rows(rw_g_up[l], 0, lora_g_pad).astype(BF16),
            k_k=rw_k_k[l], k_a=rw_k_a[l], r_k=rw_r_k[l], ln_w=rw_ln_w[l], ln_b=rw_ln_b[l],
            conv_w=conv_w[l], ffn_conv_w=ffn_conv_w[l],
            g_post_mix=g_post_mix[l], g_pre_ffn=g_pre_ffn[l],
            rdim=rdim, rw_proj=rw_proj, lora_ab=lora_ab)

    def attn_prompt(p_qkv):
        return _moba_prompt(p_qkv, nb_p, s_len, n_heads, cos_p, sin_p)

    def make_attn_sample(l):
        def fn(p_qkv):
            q_s, k_s, v_s = _rope_sample(p_qkv, nb_s, t_s, n_heads, cos_s, sin_s)
            means = _block_means(cache_k, l, page_table, n_full)
            idx = _topk_sample(q_s, means, n_full)
            idx_flat = idx[..., :MOBA_TOPK].reshape(-1)
            y = _attn_sample(q_s, k_s, v_s, cache_k, cache_v, l, idx_flat, page_table)
            return y.astype(BF16), k_s, v_s
        return fn

    outs_p, outs_s = [], []
    hp = hs = None
    fo_p = fo_s = gtf_p = gtf_s = None
    g_post_prev = None
    for l in range(depth):
        ada = _ada(c_all, w_ada[l], b_ada[l])
        mods_p = _mods(ada[:nb_p], d, False, s_len)
        mods_s = _mods(ada[nb_p:n_c], d, True, t_s)
        if l == 0:
            hp = _norm_mod(xp, g_pre_mix[l], mods_p[1], mods_p[0], tiles_p["ew"], tiles_p["tiles_per_mod"])
            hs = _norm_mod(xs, g_pre_mix[l], mods_s[1], mods_s[0], tiles_s["ew"], 1)
        else:
            xp, hp = _resid(fo_p, xp, g_post_prev, gtf_p, tiles_p["ew"], tiles_p["tiles_per_mod"],
                            nxt=(g_pre_mix[l], mods_p[1], mods_p[0]))
            xs, hs = _resid(fo_s, xs, g_post_prev, gtf_s, tiles_s["ew"], 1,
                            nxt=(g_pre_mix[l], mods_s[1], mods_s[0]))
        lw = layer_weights(l)
        xp, fo_p, gtf_p, st_p = _layer(xp, hp, mods_p, nb_p, s_len, lw, attn_prompt, zeros_p, tiles_p)
        states_s = (jnp.pad(state_shift[l], ((0, 0), (0, rw_pad - rw_proj)))[:, None, :], state_wkv[l],
                    state_conv[l], state_ffn[l])
        xs, fo_s, gtf_s, st_s = _layer(xs, hs, mods_s, nb_s, t_s, lw, make_attn_sample(l), states_s, tiles_s)
        g_post_prev = g_post_ffn[l]
        outs_p.append(st_p)
        outs_s.append(st_s)
    xp, _ = _resid(fo_p, xp, g_post_prev, gtf_p, tiles_p["ew"], tiles_p["tiles_per_mod"])
    xs, _ = _resid(fo_s, xs, g_post_prev, gtf_s, tiles_s["ew"], 1)

    stack = lambda sts, i: jnp.stack([st[i] for st in sts])
    return (xp.reshape(nb_p, s_len, d), xs.reshape(nb_s, t_s, d),
            stack(outs_p, 0), stack(outs_p, 1), stack(outs_s, 0), stack(outs_s, 1),
            stack(outs_p, 2), stack(outs_s, 2), stack(outs_p, 3), stack(outs_s, 3),
            stack(outs_p, 4), stack(outs_s, 4), stack(outs_p, 5), stack(outs_s, 5))
```

```python
import functools
import math

import jax
import jax.numpy as jnp
from jax import lax
from jax.experimental import pallas as pl
from jax.experimental.pallas import tpu as pltpu

F32 = jnp.float32
BF16 = jnp.bfloat16

LANES = 128
SUBLANES = 8
VMEM_LIMIT_BYTES = 56 * 1024 * 1024

HEAD_DIM = 128
RWKV_HEAD_DIM = 64
PAGE_SIZE = 128
MOBA_BLOCK = 256
MOBA_TOPK = 3
PAGES_PER_BLOCK = MOBA_BLOCK // PAGE_SIZE
ROPE_THETA = 10000.0
RMS_EPS = 1e-6
GN_EPS = 64e-5
NEG = -1e30
ATT_SCALE = HEAD_DIM ** -0.5

_NT = (((1,), (1,)), ((), ()))


def _cparams(*sem):
    return pltpu.CompilerParams(dimension_semantics=sem, vmem_limit_bytes=VMEM_LIMIT_BYTES)


def _round_up(x, m):
    return (x + m - 1) // m * m


def _pick_tile(n, pref):
    if n <= pref:
        return n
    t = pref
    while t >= LANES:
        if n % t == 0:
            return t
        t -= LANES
    return n


def _sigmoid(x):
    return 1.0 / (1.0 + jnp.exp(-x))


def _split_bf16(x):
    hi = x.astype(BF16)
    lo = (x - hi.astype(F32)).astype(BF16)
    return hi, lo


def _dot3_nt(a, b):
    ah, al = _split_bf16(a)
    bh, bl = _split_bf16(b)
    dg = functools.partial(lax.dot_general, dimension_numbers=_NT, preferred_element_type=F32)
    return dg(ah, bh) + dg(ah, bl) + dg(al, bh)


def _top3_lowest_index(gate, cand, lane):
    sel = jnp.zeros(gate.shape, F32)
    lane = lane.astype(F32)
    firsts = []
    for _ in range(MOBA_TOPK):
        gm = jnp.where(cand > 0.5, gate, -jnp.inf)
        mx = jnp.max(gm, axis=1, keepdims=True)
        is_max = jnp.where(cand > 0.5, jnp.where(gm == mx, 1.0, 0.0), 0.0)
        first = jnp.min(jnp.where(is_max > 0.5, lane, float(LANES)), axis=1, keepdims=True)
        pick = jnp.where(lane == first, 1.0, 0.0)
        sel = sel + pick
        cand = cand - pick
        firsts.append(first)
    return sel, firsts


def _ada_body(c_ref, w_ref, b_ref, o_ref):
    c = c_ref[...]
    s = c * _sigmoid(c)
    o_ref[...] = jnp.dot(s.astype(BF16), w_ref[...].astype(BF16),
                         preferred_element_type=F32) + b_ref[...]


def _ada(c_pad, w_all, layer, b):
    _, d, n = w_all.shape
    tn = _pick_tile(n, 512)
    return pl.pallas_call(
        _ada_body,
        grid=(n // tn,),
        in_specs=[pl.BlockSpec(c_pad.shape, lambda j: (0, 0)),
                  pl.BlockSpec((None, d, tn), lambda j: (layer, 0, j)),
                  pl.BlockSpec((1, tn), lambda j: (0, j))],
        out_specs=pl.BlockSpec((c_pad.shape[0], tn), lambda j: (0, j)),
        out_shape=jax.ShapeDtypeStruct((c_pad.shape[0], n), F32),
        compiler_params=_cparams("parallel"),
        name="ada",
    )(c_pad, w_all, b.reshape(1, n))


def _rms(x, g):
    ms = jnp.mean(x * x, axis=-1, keepdims=True)
    return x * lax.rsqrt(ms + RMS_EPS) * g


def _norm_mod_body(x_ref, g_ref, sc_ref, sh_ref, h_ref):
    h_ref[...] = (_rms(x_ref[...], g_ref[...]) * sc_ref[...] + sh_ref[...]).astype(h_ref.dtype)


def _mod_spec(mod, tiles_per_mod):
    return pl.BlockSpec((None,) + mod.shape[1:], lambda i: (i // tiles_per_mod, 0, 0))


def _norm_mod(x, g, sc1p, sh, tm, tiles_per_mod):
    m, d = x.shape
    row = pl.BlockSpec((tm, d), lambda i: (i, 0))
    return pl.pallas_call(
        _norm_mod_body,
        grid=(m // tm,),
        in_specs=[row, pl.BlockSpec((1, d), lambda i: (0, 0)),
                  _mod_spec(sc1p, tiles_per_mod), _mod_spec(sh, tiles_per_mod)],
        out_specs=row,
        out_shape=jax.ShapeDtypeStruct((m, d), BF16),
        compiler_params=_cparams("parallel"),
        name="norm_mod",
    )(x, g.reshape(1, d), sc1p, sh)


def _resid_body(*refs, with_h):
    if with_h:
        y_ref, x_ref, gp_ref, gate_ref, gn_ref, sc_ref, sh_ref, xo_ref, h_ref = refs
    else:
        y_ref, x_ref, gp_ref, gate_ref, xo_ref = refs
    xn = x_ref[...] + gate_ref[...] * _rms(y_ref[...], gp_ref[...])
    xo_ref[...] = xn
    if with_h:
        h_ref[...] = (_rms(xn, gn_ref[...]) * sc_ref[...] + sh_ref[...]).astype(h_ref.dtype)


def _resid(y, x, g_post, gate1p, tm, tiles_per_mod, nxt=None):
    m, d = x.shape
    row = pl.BlockSpec((tm, d), lambda i: (i, 0))
    vec = pl.BlockSpec((1, d), lambda i: (0, 0))
    ins = [y, x, g_post.reshape(1, d), gate1p]
    specs = [row, row, vec, _mod_spec(gate1p, tiles_per_mod)]
    out_shape = [jax.ShapeDtypeStruct((m, d), F32)]
    out_specs = [row]
    if nxt is not None:
        g_next, sc1p, sh = nxt
        ins += [g_next.reshape(1, d), sc1p, sh]
        specs += [vec, _mod_spec(sc1p, tiles_per_mod), _mod_spec(sh, tiles_per_mod)]
        out_shape.append(jax.ShapeDtypeStruct((m, d), BF16))
        out_specs.append(row)
    res = pl.pallas_call(
        functools.partial(_resid_body, with_h=nxt is not None),
        grid=(m // tm,),
        in_specs=specs,
        out_specs=out_specs,
        out_shape=out_shape,
        compiler_params=_cparams("parallel"),
        name="resid",
    )(*ins)
    return res if nxt is not None else (res[0], None)


def _mm_body(*refs, k_sizes, cast_w):
    n_lhs = len(k_sizes)
    w_ref, o_ref = refs[n_lhs], refs[n_lhs + 1]
    if cast_w:
        wbf = refs[n_lhs + 2]

        @pl.when(pl.program_id(1) == 0)
        def _():
            wbf[...] = w_ref[...].astype(BF16)

        w_ref = wbf
    acc = None
    off = 0
    for a_ref, ks in zip(refs[:n_lhs], k_sizes):
        part = jnp.dot(a_ref[...], w_ref[off:off + ks, :], preferred_element_type=F32)
        acc = part if acc is None else acc + part
        off += ks
    o_ref[...] = acc.astype(o_ref.dtype)


def _matmul(lhs_list, w_all, layer, col0, n, tm, tn):
    m = lhs_list[0].shape[0]
    k_sizes = tuple(a.shape[1] for a in lhs_list)
    k = w_all.shape[1]
    assert sum(k_sizes) == k and m % tm == 0 and n % tn == 0 and col0 % tn == 0
    assert col0 + n <= w_all.shape[2]
    cast_w = w_all.dtype != BF16
    j0 = col0 // tn
    return pl.pallas_call(
        functools.partial(_mm_body, k_sizes=k_sizes, cast_w=cast_w),
        grid=(n // tn, m // tm),
        in_specs=[pl.BlockSpec((tm, ks), lambda j, i: (i, 0)) for ks in k_sizes]
        + [pl.BlockSpec((None, k, tn), lambda j, i: (layer, 0, j0 + j))],
        out_specs=pl.BlockSpec((tm, tn), lambda j, i: (i, j)),
        out_shape=jax.ShapeDtypeStruct((m, n), F32),
        scratch_shapes=[pltpu.VMEM((k, tn), BF16)] if cast_w else [],
        compiler_params=_cparams("parallel", "arbitrary"),
        name="matmul",
    )(*lhs_list, w_all)


def _shifted_rows(u, halo, state, first):
    p1 = jnp.where(first, state[1:2, :], halo[SUBLANES - 1:SUBLANES, :])
    p2 = jnp.where(first, state[0:1, :], halo[SUBLANES - 2:SUBLANES - 1, :])
    row = lax.broadcasted_iota(jnp.int32, u.shape, 0)
    u1 = jnp.where(row == 0, p1, pltpu.roll(u, 1, 0))
    u2 = jnp.where(row == 0, p2, jnp.where(row == 1, p1, pltpu.roll(u, 2, 0)))
    return u1, u2


def _shifted_rows_short(u, prev1, prev2, t_len):
    pos = lax.broadcasted_iota(jnp.int32, u.shape, 0) % t_len
    u1 = jnp.where(pos == 0, prev1, pltpu.roll(u, 1, 0))
    u2 = jnp.where(pos == 0, prev2, jnp.where(pos == 1, prev1, pltpu.roll(u, 2, 0)))
    return u1, u2


def _ffn_in_body(*refs, tiles_per_seq, t_len):
    short = tiles_per_seq == 0
    if short:
        h_ref, wg_ref, wu_ref, p1_ref, p2_ref, cw_ref, f_ref, u_ref, wgb, wub = refs
    else:
        h_ref, wg_ref, wu_ref, s0_ref, cw_ref, f_ref, s1_ref, wgb, wub, tail = refs
    i = pl.program_id(1)

    @pl.when(i == 0)
    def _():
        wgb[...] = wg_ref[...].astype(BF16)
        wub[...] = wu_ref[...].astype(BF16)
        if not short:
            tail[...] = jnp.zeros(tail.shape, F32)

    h = h_ref[...]
    u = jnp.dot(h, wgb[...], preferred_element_type=F32)
    up = jnp.dot(h, wub[...], preferred_element_type=F32)
    if short:
        u1, u2 = _shifted_rows_short(u, p1_ref[...], p2_ref[...], t_len)
        u_ref[...] = u
    else:
        u1, u2 = _shifted_rows(u, tail[...], s0_ref[...], (i % tiles_per_seq) == 0)
        tail[...] = u[u.shape[0] - SUBLANES:, :]

        @pl.when((i % tiles_per_seq) == tiles_per_seq - 1)
        def _():
            s1_ref[...] = u[u.shape[0] - 2:, :]

    w = cw_ref[...]
    y = u2 * w[0:1, :] + u1 * w[1:2, :] + u * w[2:3, :]
    f_ref[...] = (y * _sigmoid(y) * up).astype(f_ref.dtype)


def _ffn_in(h, w_all, layer, ffn0, ffn_w, tm, tc, t_len):
    m, k = h.shape
    dff = ffn_w.shape[1]
    nseq = ffn0.shape[0]
    ncol = dff // tc
    assert dff % tc == 0 and m % tm == 0
    short = tm > t_len
    tiles_per_seq = 0 if short else t_len // tm
    lhs = pl.BlockSpec((tm, k), lambda j, i: (i, 0))
    wspec = lambda off: pl.BlockSpec((None, k, tc), lambda j, i: (layer, 0, off + j))
    tile = pl.BlockSpec((tm, tc), lambda j, i: (i, j))
    cw = pl.BlockSpec((3, tc), lambda j, i: (0, j))
    scratch = [pltpu.VMEM((k, tc), BF16), pltpu.VMEM((k, tc), BF16)]
    if short:
        assert tm == m and m == nseq * t_len
        prev1 = jnp.repeat(ffn0[:, 1, :], t_len, axis=0)
        prev2 = jnp.repeat(ffn0[:, 0, :], t_len, axis=0)
        f, u = pl.pallas_call(
            functools.partial(_ffn_in_body, tiles_per_seq=0, t_len=t_len),
            grid=(ncol, 1),
            in_specs=[lhs, wspec(0), wspec(ncol), tile, tile, cw],
            out_specs=[tile, tile],
            out_shape=[jax.ShapeDtypeStruct((m, dff), BF16), jax.ShapeDtypeStruct((m, dff), F32)],
            scratch_shapes=scratch,
            compiler_params=_cparams("parallel", "arbitrary"),
            name="ffn_in",
        )(h, w_all, w_all, prev1, prev2, ffn_w)
        return f, u.reshape(nseq, t_len, dff)[:, t_len - 2:, :]
    state = pl.BlockSpec((None, 2, tc), lambda j, i: (i // tiles_per_seq, 0, j))
    return pl.pallas_call(
        functools.partial(_ffn_in_body, tiles_per_seq=tiles_per_seq, t_len=t_len),
        grid=(ncol, m // tm),
        in_specs=[lhs, wspec(0), wspec(ncol), state, cw],
        out_specs=[tile, state],
        out_shape=[jax.ShapeDtypeStruct((m, dff), BF16), jax.ShapeDtypeStruct((nseq, 2, dff), F32)],
        scratch_shapes=scratch + [pltpu.VMEM((SUBLANES, tc), F32)],
        compiler_params=_cparams("parallel", "arbitrary"),
        name="ffn_in",
    )(h, w_all, w_all, ffn0, ffn_w)


def _rope_tables(pos):
    half = HEAD_DIM // 2
    inv = ROPE_THETA ** (-jnp.arange(half, dtype=F32) / half)
    ang = pos.astype(F32)[:, None] * inv[None, :]
    cos, sin = jnp.cos(ang), jnp.sin(ang)
    return jnp.concatenate([cos, cos], axis=1), jnp.concatenate([-sin, sin], axis=1)


def _rope(x, cos2, sin2):
    return x * cos2 + pltpu.roll(x, HEAD_DIM // 2, 1) * sin2


def _moba_p_body(q_ref, k_ref, v_ref, cos_ref, sin_ref, e_ref, y_ref, ko_ref, vo_ref, kext, vbf, means):
    s_len = k_ref.shape[0]
    nb = s_len // MOBA_BLOCK
    n_pages = s_len // PAGE_SIZE

    kr = _rope(k_ref[...], cos_ref[...], sin_ref[...])
    v = v_ref[...]
    ko_ref[...] = kr.reshape(n_pages, PAGE_SIZE, HEAD_DIM)
    vo_ref[...] = v.reshape(n_pages, PAGE_SIZE, HEAD_DIM)
    kext[:, 0:HEAD_DIM] = kr.astype(BF16)
    kext[:, HEAD_DIM:] = e_ref[...]
    vbf[...] = v.astype(BF16)
    means[...] = jnp.zeros(means.shape, F32)
    means[0:nb, :] = jnp.mean(kr.reshape(nb, MOBA_BLOCK, HEAD_DIM), axis=1)

    lane = lax.broadcasted_iota(jnp.int32, (MOBA_BLOCK, LANES), 1)
    row = lax.broadcasted_iota(jnp.int32, (MOBA_BLOCK, MOBA_BLOCK), 0)
    col = lax.broadcasted_iota(jnp.int32, (MOBA_BLOCK, MOBA_BLOCK), 1)
    for n in range(nb):
        rows = slice(n * MOBA_BLOCK, (n + 1) * MOBA_BLOCK)
        qr = _rope(q_ref[rows, :], cos_ref[rows, :], sin_ref[rows, :])
        keep = jnp.where(lane == n, 1.0, 0.0)
        if n > 0:
            gate = _dot3_nt(qr, means[...])
            cand = jnp.where(lane < n, 1.0, 0.0)
            sel, _ = _top3_lowest_index(gate, cand, lane)
            keep = keep + sel
        qext = jnp.concatenate([qr.astype(BF16), (keep - 1.0).astype(BF16)], axis=1)
        end = (n + 1) * MOBA_BLOCK
        s = lax.dot_general(qext, kext[0:end, :], _NT, preferred_element_type=F32) * ATT_SCALE
        s_own = jnp.where(col <= row, s[:, n * MOBA_BLOCK:end], NEG)
        mx = jnp.max(s_own, axis=1, keepdims=True)
        if n > 0:
            s_past = s[:, 0:n * MOBA_BLOCK]
            mx = jnp.maximum(mx, jnp.max(s_past, axis=1, keepdims=True))
        p_own = jnp.exp(s_own - mx)
        l = jnp.sum(p_own, axis=1, keepdims=True)
        o = jnp.dot(p_own.astype(BF16), vbf[n * MOBA_BLOCK:end, :], preferred_element_type=F32)
        if n > 0:
            p_past = jnp.exp(s_past - mx)
            l = l + jnp.sum(p_past, axis=1, keepdims=True)
            o = o + jnp.dot(p_past.astype(BF16), vbf[0:n * MOBA_BLOCK, :], preferred_element_type=F32)
        y_ref[rows, :] = (o / l).astype(y_ref.dtype)


MASK_BIG = 1e30


def _moba_prompt(p_qkv, nseq, s_len, n_heads, cos2, sin2):
    n_pages = s_len // PAGE_SIZE
    m = nseq * s_len
    e = jnp.where(jnp.arange(s_len)[:, None] // MOBA_BLOCK == jnp.arange(LANES)[None, :],
                  MASK_BIG, 0.0).astype(BF16)
    full = lambda off: pl.BlockSpec((s_len, HEAD_DIM), lambda b, h: (b, off + h))
    tab = pl.BlockSpec((s_len, HEAD_DIM), lambda b, h: (0, 0))
    pages = pl.BlockSpec((None, n_pages, None, PAGE_SIZE, HEAD_DIM), lambda b, h: (b, 0, h, 0, 0))
    page_shape = jax.ShapeDtypeStruct((nseq, n_pages, n_heads, PAGE_SIZE, HEAD_DIM), F32)
    return pl.pallas_call(
        _moba_p_body,
        grid=(nseq, n_heads),
        in_specs=[full(0), full(n_heads), full(2 * n_heads), tab, tab, tab],
        out_specs=[full(0), pages, pages],
        out_shape=[jax.ShapeDtypeStruct((m, n_heads * HEAD_DIM), BF16), page_shape, page_shape],
        scratch_shapes=[pltpu.VMEM((s_len, HEAD_DIM + LANES), BF16), pltpu.VMEM((s_len, HEAD_DIM), BF16),
                        pltpu.VMEM((LANES, HEAD_DIM), F32)],
        compiler_params=_cparams("parallel", "parallel"),
        name="moba_prompt",
    )(p_qkv, p_qkv, p_qkv, cos2, sin2, e)


def _rope_s_body(p_ref, cos_ref, sin_ref, q_o, k_o, v_o, *, n_heads):
    cos2, sin2 = cos_ref[...], sin_ref[...]
    for h in range(n_heads):
        sl = lambda part: p_ref[:, (part * n_heads + h) * HEAD_DIM:(part * n_heads + h + 1) * HEAD_DIM]
        q_o[h] = _rope(sl(0), cos2, sin2)
        k_o[h] = _rope(sl(1), cos2, sin2)
        v_o[h] = sl(2)


def _rope_sample(p_qkv, nseq, t_len, n_heads, cos2, sin2):
    shp = jax.ShapeDtypeStruct((nseq, n_heads, t_len, HEAD_DIM), F32)
    out = pl.BlockSpec((None, n_heads, t_len, HEAD_DIM), lambda b: (b, 0, 0, 0))
    tab = pl.BlockSpec((t_len, HEAD_DIM), lambda b: (0, 0))
    return pl.pallas_call(
        functools.partial(_rope_s_body, n_heads=n_heads),
        grid=(nseq,),
        in_specs=[pl.BlockSpec((t_len, p_qkv.shape[1]), lambda b: (b, 0)), tab, tab],
        out_specs=[out, out, out],
        out_shape=[shp, shp, shp],
        compiler_params=_cparams("parallel"),
        name="rope_sample",
    )(p_qkv, cos2, sin2)


_MEANS_BLOCKS_PER_STEP = 4


def _means_body(pt_ref, *refs, n_heads):
    page_refs, o_ref = refs[:-1], refs[-1]
    j = pl.program_id(1)

    @pl.when(j == 0)
    def _():
        o_ref[...] = jnp.zeros(o_ref.shape, F32)

    for blk in range(_MEANS_BLOCKS_PER_STEP):
        s = jnp.sum(page_refs[PAGES_PER_BLOCK * blk][...], axis=1)
        for pg in range(1, PAGES_PER_BLOCK):
            s = s + jnp.sum(page_refs[PAGES_PER_BLOCK * blk + pg][...], axis=1)
        s = s * (1.0 / MOBA_BLOCK)
        for h in range(n_heads):
            o_ref[h, pl.ds(j * _MEANS_BLOCKS_PER_STEP + blk, 1), :] = s[h:h + 1, :]


def _block_means(cache_k, layer, page_table, n_full):
    nseq = page_table.shape[0]
    n_heads = cache_k.shape[2]
    pages_per_step = PAGES_PER_BLOCK * _MEANS_BLOCKS_PER_STEP
    assert n_full <= LANES and n_full % _MEANS_BLOCKS_PER_STEP == 0
    page = lambda pg: pl.BlockSpec(
        (None, None, n_heads, PAGE_SIZE, HEAD_DIM),
        lambda b, j, pt: (layer, pt[b, pages_per_step * j + pg], 0, 0, 0))
    return pl.pallas_call(
        functools.partial(_means_body, n_heads=n_heads),
        grid_spec=pltpu.PrefetchScalarGridSpec(
            num_scalar_prefetch=1,
            grid=(nseq, n_full // _MEANS_BLOCKS_PER_STEP),
            in_specs=[page(pg) for pg in range(pages_per_step)],
            out_specs=pl.BlockSpec((None, n_heads, LANES, HEAD_DIM), lambda b, j, pt: (b, 0, 0, 0)),
        ),
        out_shape=jax.ShapeDtypeStruct((nseq, n_heads, LANES, HEAD_DIM), F32),
        compiler_params=_cparams("parallel", "arbitrary"),
        name="block_means",
    )(page_table, *([cache_k] * pages_per_step))


def _topk_s_body(q_ref, m_ref, o_ref, *, n_heads, n_full):
    for h in range(n_heads):
        gate = _dot3_nt(q_ref[h], m_ref[h])
        lane = lax.broadcasted_iota(jnp.int32, gate.shape, 1)
        cand = jnp.where(lane < n_full, 1.0, 0.0)
        _, firsts = _top3_lowest_index(gate, cand, lane)
        idx = jnp.zeros(gate.shape, F32)
        for r, first in enumerate(firsts):
            idx = jnp.where(lane == r, first, idx)
        o_ref[h] = idx.astype(jnp.int32)


def _topk_sample(q_s, means, n_full):
    nseq, n_heads, t_len, _ = q_s.shape
    return pl.pallas_call(
        functools.partial(_topk_s_body, n_heads=n_heads, n_full=n_full),
        grid=(nseq,),
        in_specs=[pl.BlockSpec((None, n_heads, t_len, HEAD_DIM), lambda b: (b, 0, 0, 0)),
                  pl.BlockSpec((None, n_heads, LANES, HEAD_DIM), lambda b: (b, 0, 0, 0))],
        out_specs=pl.BlockSpec((None, n_heads, t_len, LANES), lambda b: (b, 0, 0, 0)),
        out_shape=jax.ShapeDtypeStruct((nseq, n_heads, t_len, LANES), jnp.int32),
        compiler_params=_cparams("parallel"),
        name="topk_sample",
    )(q_s, means)


_N_SEL_PAGES = MOBA_TOPK * PAGES_PER_BLOCK


def _attn_s_body(idx_ref, pt_ref, q_ref, kn_ref, vn_ref, *refs):
    q = q_ref[...]
    t_len = q.shape[0]
    n_pg = t_len * _N_SEL_PAGES
    k_refs, v_refs, o_ref = refs[:n_pg], refs[n_pg:2 * n_pg], refs[-1]
    kn, vn = kn_ref[...], vn_ref[...]
    key_new = lax.broadcasted_iota(jnp.int32, (t_len, 1), 0)
    outs = []
    for t in range(t_len):
        q_t = q[t:t + 1, :]
        pages = slice(t * _N_SEL_PAGES, (t + 1) * _N_SEL_PAGES)
        score = lambda keys: jnp.sum(keys * q_t, axis=1, keepdims=True) * ATT_SCALE
        s_all = [score(k[...]) for k in k_refs[pages]]
        s_all.append(jnp.where(key_new <= t, score(kn), NEG))
        mx = jnp.max(s_all[0], axis=0, keepdims=True)
        for s in s_all[1:]:
            mx = jnp.maximum(mx, jnp.max(s, axis=0, keepdims=True))
        l = jnp.zeros((1, 1), F32)
        o = jnp.zeros((1, HEAD_DIM), F32)
        for s, v in zip(s_all, [v[...] for v in v_refs[pages]] + [vn]):
            p = jnp.exp(s - mx)
            l = l + jnp.sum(p, axis=0, keepdims=True)
            o = o + jnp.sum(p * v, axis=0, keepdims=True)
        outs.append(o / l)
    o_ref[...] = jnp.concatenate(outs, axis=0)


def _attn_sample(q_s, k_new, v_new, cache_k, cache_v, layer, idx_flat, page_table):
    nseq, n_heads, t_len, _ = q_s.shape

    def page(t, s, pg):
        def imap(b, h, idx, pt):
            blk = idx[((b * n_heads + h) * t_len + t) * MOBA_TOPK + s]
            return (layer, pt[b, blk * PAGES_PER_BLOCK + pg], h, 0, 0)
        return pl.BlockSpec((None, None, None, PAGE_SIZE, HEAD_DIM), imap)

    pages = [page(t, s, pg) for t in range(t_len) for s in range(MOBA_TOPK) for pg in range(PAGES_PER_BLOCK)]
    new = pl.BlockSpec((None, None, t_len, HEAD_DIM), lambda b, h, idx, pt: (b, h, 0, 0))
    return pl.pallas_call(
        _attn_s_body,
        grid_spec=pltpu.PrefetchScalarGridSpec(
            num_scalar_prefetch=2,
            grid=(nseq, n_heads),
            in_specs=[new, new, new] + pages + pages,
            out_specs=pl.BlockSpec((t_len, HEAD_DIM), lambda b, h, idx, pt: (b, h)),
        ),
        out_shape=jax.ShapeDtypeStruct((nseq * t_len, n_heads * HEAD_DIM), F32),
        compiler_params=_cparams("parallel", "parallel"),
        name="attn_sample",
    )(idx_flat, page_table, q_s, k_new, v_new, *([cache_k] * len(pages)), *([cache_v] * len(pages)))


def _rw_prep_body(p_ref, halo_ref, sp_ref, mu_ref, w0_ref, wup_ref, a0_ref, aup_ref, gup_ref,
                  r_o, k_o, v_o, a_o, d_o, g_o, *, tiles_per_seq, rdim, lora_ab):
    i = pl.program_id(0)
    pf = p_ref[...]
    first = (i % tiles_per_seq) == 0
    prev0 = jnp.where(first, sp_ref[...], halo_ref[SUBLANES - 1:SUBLANES, :])
    row = lax.broadcasted_iota(jnp.int32, pf.shape, 0)
    prev = jnp.where(row == 0, prev0, pltpu.roll(pf, 1, 0))
    pm = pf + (prev - pf) * mu_ref[...]
    r, k, v = pm[:, 0:rdim], pm[:, rdim:2 * rdim], pm[:, 2 * rdim:3 * rdim]
    ca = pm[:, 3 * rdim:3 * rdim + lora_ab]
    cb = pm[:, 3 * rdim + lora_ab:]
    w_lin = jnp.dot(jnp.tanh(ca).astype(BF16), wup_ref[...], preferred_element_type=F32)
    a_lin = jnp.dot(ca.astype(BF16), aup_ref[...], preferred_element_type=F32)
    g = jnp.dot(_sigmoid(cb).astype(BF16), gup_ref[...], preferred_element_type=F32)
    y = -(w0_ref[...] + w_lin)
    w = -(jnp.maximum(y, 0.0) + jnp.log(1.0 + jnp.exp(-jnp.abs(y)))) - 0.5
    a = _sigmoid(a0_ref[...] + a_lin)
    r_o[...] = r
    k_o[...] = k
    v_o[...] = v
    a_o[...] = a
    d_o[...] = jnp.exp(-jnp.exp(w))
    g_o[...] = g


def _rw_prep(p_rw, shift_prev, mu, w0, wup, a0, aup, gup, tm, tiles_per_seq, rdim, lora_ab):
    m, wp = p_rw.shape
    row = pl.BlockSpec((tm, wp), lambda i: (i, 0))
    halo = pl.BlockSpec((SUBLANES, wp), lambda i: (jnp.maximum(i * (tm // SUBLANES) - 1, 0), 0))
    const = lambda a: pl.BlockSpec(a.shape, lambda i: (0,) * a.ndim)
    out = pl.BlockSpec((tm, rdim), lambda i: (i, 0))
    shp = jax.ShapeDtypeStruct((m, rdim), F32)
    vecs = [x.reshape(1, -1) for x in (mu, w0)]
    return pl.pallas_call(
        functools.partial(_rw_prep_body, tiles_per_seq=tiles_per_seq, rdim=rdim, lora_ab=lora_ab),
        grid=(m // tm,),
        in_specs=[row, halo, pl.BlockSpec((None, 1, wp), lambda i: (i // tiles_per_seq, 0, 0)),
                  const(vecs[0]), const(vecs[1]), const(wup), const(a0.reshape(1, -1)), const(aup),
                  const(gup)],
        out_specs=[out] * 6,
        out_shape=[shp] * 6,
        compiler_params=_cparams("parallel"),
        name="rwkv_prep",
    )(p_rw, p_rw, shift_prev, vecs[0], vecs[1], wup, a0.reshape(1, -1), aup, gup)


_SCAN_ROWS_IN_FLIGHT = 4


def _rw_scan_body(r_ref, k_ref, a_ref, d_ref, v_ref, kkt_ref, kat_ref, rk_ref, lnw_ref, lnb_ref, s0_ref,
                  o_ref, so_ref, state, b_s, c_s, kp_s, bd_s, al_s, be_s, rk_s, o_s, *, halves, chains):
    pid = pl.program_id(0)
    tc, nv = v_ref.shape[0], v_ref.shape[1]
    n_val = nv * halves
    rows = min(_SCAN_ROWS_IN_FLIGHT, nv)

    @pl.when(pid == 0)
    def _():
        state[...] = s0_ref[...]

    k, a = k_ref[...], a_ref[...]
    kk = k * kkt_ref[...]
    kkn = kk / jnp.maximum(jnp.sqrt(jnp.sum(kk * kk, axis=1, keepdims=True)), 1e-12)
    kp = k * (1.0 + (a - 1.0) * kat_ref[...])
    b = -kkn
    c = kkn * a
    b_s[...] = b
    c_s[...] = c
    kp_s[...] = kp
    rk_s[...] = jnp.sum(r_ref[...] * kp * rk_ref[...], axis=1, keepdims=True)
    pair = lambda x: x.reshape((tc // 2, 2) + x.shape[1:])
    b1 = pair(b)[:, 1]
    bd_s[...] = pair(d_ref[...])[:, 0] * b1
    al_s[...] = jnp.sum(pair(c)[:, 0] * b1, axis=1, keepdims=True)
    be_s[...] = jnp.sum(pair(kp)[:, 0] * b1, axis=1, keepdims=True)

    for g in range(nv // rows):
        def step(s, s_rows, g=g):
            t0, t1 = 2 * s, 2 * s + 1
            b0, bd, al, be = b_s[t0], bd_s[s], al_s[s], be_s[s]
            c0, kp0, d0, r0 = c_s[t0], kp_s[t0], d_ref[t0], r_ref[t0]
            c1, kp1, d1, r1 = c_s[t1], kp_s[t1], d_ref[t1], r_ref[t1]
            new = []
            for i, sv in enumerate(s_rows):
                vi = g * rows + i
                v0, v1 = v_ref[t0, vi:vi + 1, :], v_ref[t1, vi:vi + 1, :]
                u0 = jnp.sum(sv * b0, axis=0, keepdims=True)
                u1 = jnp.sum(sv * bd, axis=0, keepdims=True) + u0 * al + v0 * be
                s1 = sv * d0 + u0 * c0 + v0 * kp0
                o_s[t0, vi:vi + 1, :] = jnp.sum(s1 * r0, axis=0, keepdims=True)
                s2 = s1 * d1 + u1 * c1 + v1 * kp1
                o_s[t1, vi:vi + 1, :] = jnp.sum(s2 * r1, axis=0, keepdims=True)
                new.append(s2)
            return tuple(new)

        s_rows = lax.fori_loop(0, tc // 2, step, tuple(state[g * rows + i] for i in range(rows)))
        for i, sv in enumerate(s_rows):
            state[g * rows + i] = sv

    def head_sum(x):
        x = x.reshape(tc * nv, LANES)
        shift = chains
        for _ in range(int(math.log2(halves))):
            x = x + pltpu.roll(x, shift, 1)
            shift *= 2
        return jnp.sum(x.reshape(tc, nv, LANES), axis=1, keepdims=True)

    o = o_s[...]
    dev = o - head_sum(o) * (1.0 / n_val)
    var = head_sum(dev * dev) * (1.0 / n_val)
    on = dev * lax.rsqrt(var + GN_EPS) * lnw_ref[...] + lnb_ref[...]
    o_ref[...] = on + rk_s[...] * v_ref[...]

    @pl.when(pid == pl.num_programs(0) - 1)
    def _():
        so_ref[...] = state[...]


def _rw_scan(r, k, a, d, v, kk_tile, ka_tile, rk_tile, lnw_tile, lnb_tile, s0, tc, halves, chains):
    t_len, kdim, _ = r.shape
    nv = v.shape[1]
    assert nv % min(_SCAN_ROWS_IN_FLIGHT, nv) == 0 and t_len % tc == 0 and tc % 2 == 0
    ktile = pl.BlockSpec((tc, kdim, LANES), lambda i: (i, 0, 0))
    vtile = pl.BlockSpec((tc, nv, LANES), lambda i: (i, 0, 0))
    const = lambda x: pl.BlockSpec(x.shape, lambda i: (0,) * x.ndim)
    consts = [kk_tile, ka_tile, rk_tile, lnw_tile, lnb_tile, s0]
    return pl.pallas_call(
        functools.partial(_rw_scan_body, halves=halves, chains=chains),
        grid=(t_len // tc,),
        in_specs=[ktile] * 4 + [vtile] + [const(x) for x in consts],
        out_specs=[vtile, const(s0)],
        out_shape=[jax.ShapeDtypeStruct(v.shape, F32), jax.ShapeDtypeStruct(s0.shape, F32)],
        scratch_shapes=[pltpu.VMEM(s0.shape, F32)] + [pltpu.VMEM((tc, kdim, LANES), F32)] * 3
        + [pltpu.VMEM((tc // 2, kdim, LANES), F32)] + [pltpu.VMEM((tc // 2, 1, LANES), F32)] * 2
        + [pltpu.VMEM((tc, 1, LANES), F32), pltpu.VMEM((tc, nv, LANES), F32)],
        compiler_params=_cparams("arbitrary"),
        name="rwkv_scan",
    )(r, k, a, d, v, *consts)


def _conv_mix_body(gb_ref, gc_ref, hc_ref, gch_ref, hch_ref, c0_ref, w_ref, o_ref, g_ref,
                   ycv_ref, yrw_ref, c1_ref, *, tiles_per_seq):
    i = pl.program_id(0)
    first = (i % tiles_per_seq) == 0
    u = gc_ref[...] * hc_ref[...]
    u1, u2 = _shifted_rows(u, gch_ref[...] * hch_ref[...], c0_ref[...], first)
    w = w_ref[...]
    y = u2 * w[0:1, :] + u1 * w[1:2, :] + u * w[2:3, :]
    ycv_ref[...] = (gb_ref[...] * y).astype(ycv_ref.dtype)
    yrw_ref[...] = (o_ref[...] * g_ref[...]).astype(yrw_ref.dtype)

    @pl.when((i % tiles_per_seq) == tiles_per_seq - 1)
    def _():
        c1_ref[...] = u[u.shape[0] - 2:, :]


def _conv_mix(p_cv, conv0, conv_w, o_rw, g_rw, tm, tiles_per_seq):
    m = p_cv.shape[0]
    c = conv_w.shape[1]
    nseq = conv0.shape[0]
    col = lambda j: pl.BlockSpec((tm, c), lambda i: (i, j))
    halo = lambda j: pl.BlockSpec((SUBLANES, c), lambda i: (jnp.maximum(i * (tm // SUBLANES) - 1, 0), j))
    state = pl.BlockSpec((None, 2, c), lambda i: (i // tiles_per_seq, 0, 0))
    rw = pl.BlockSpec((tm, o_rw.shape[1]), lambda i: (i, 0))
    return pl.pallas_call(
        functools.partial(_conv_mix_body, tiles_per_seq=tiles_per_seq),
        grid=(m // tm,),
        in_specs=[col(0), col(1), col(2), halo(1), halo(2), state,
                  pl.BlockSpec((3, c), lambda i: (0, 0)), rw, rw],
        out_specs=[col(0), rw, state],
        out_shape=[jax.ShapeDtypeStruct((m, c), BF16), jax.ShapeDtypeStruct(o_rw.shape, BF16),
                   jax.ShapeDtypeStruct((nseq, 2, c), F32)],
        compiler_params=_cparams("arbitrary"),
        name="conv_mix",
    )(p_cv, p_cv, p_cv, p_cv, p_cv, conv0, conv_w, o_rw, g_rw)


def _chain_geometry(nseq, n_heads):
    chains = nseq * n_heads
    assert LANES % chains == 0
    halves = LANES // chains
    assert RWKV_HEAD_DIM % halves == 0
    return chains, halves, RWKV_HEAD_DIM // halves


def _to_chain_k(x, nseq, t_len, n_heads, halves):
    x = x.reshape(nseq, t_len, n_heads, RWKV_HEAD_DIM).transpose(1, 3, 0, 2)
    return jnp.tile(x.reshape(t_len, RWKV_HEAD_DIM, nseq * n_heads), (1, 1, halves))


def _to_chain_v(x, nseq, t_len, n_heads, halves, nv):
    x = x.reshape(nseq, t_len, n_heads, halves, nv).transpose(1, 4, 3, 0, 2)
    return x.reshape(t_len, nv, LANES)


def _from_chain_v(x, nseq, t_len, n_heads, halves, nv):
    x = x.reshape(t_len, nv, halves, nseq, n_heads).transpose(3, 0, 4, 2, 1)
    return x.reshape(nseq * t_len, n_heads * RWKV_HEAD_DIM)


def _state_to_chain(s, nseq, n_heads, halves, nv):
    s = s.reshape(nseq, n_heads, halves, nv, RWKV_HEAD_DIM).transpose(3, 4, 2, 0, 1)
    return s.reshape(nv, RWKV_HEAD_DIM, LANES)


def _state_from_chain(s, nseq, n_heads, halves, nv):
    s = s.reshape(nv, RWKV_HEAD_DIM, halves, nseq, n_heads).transpose(3, 4, 2, 0, 1)
    return s.reshape(nseq, n_heads, RWKV_HEAD_DIM, RWKV_HEAD_DIM)


def _head_tile_k(x, nseq, halves):
    return jnp.tile(x.T, (1, halves * nseq))


def _head_tile_v(x, nseq, n_heads, halves, nv):
    x = x.reshape(n_heads, halves, nv).transpose(2, 1, 0)
    return jnp.broadcast_to(x[:, :, None, :], (nv, halves, nseq, n_heads)).reshape(nv, LANES)


def _mods(ada_rows, d, per_row, rows_per_seq):
    parts = [ada_rows[:, i * d:(i + 1) * d] for i in range(6)]
    sh_m, sc_m, gt_m, sh_f, sc_f, gt_f = parts
    out = (sh_m, 1.0 + sc_m, 1.0 + gt_m, sh_f, 1.0 + sc_f, 1.0 + gt_f)
    if per_row:
        return tuple(jnp.repeat(z, rows_per_seq, axis=0)[None] for z in out)
    return tuple(z[:, None, :] for z in out)


def _layer(x, h_in, mods, nseq, t_len, lw, attn_fn, states, tiles):
    (sh_m, sc_m, gt_m, sh_f, sc_f, gt_f) = mods
    shift0, wkv0, conv0, ffn0 = states
    d = x.shape[1]
    tm_mm, tm_ew, tm_seq, tpm = tiles["mm"], tiles["ew"], tiles["seq"], tiles["tiles_per_mod"]
    tiles_per_seq = t_len // tm_seq
    rdim = lw["rdim"]
    n_rw_heads = rdim // RWKV_HEAD_DIM

    l = lw["layer"]
    att3, rw_pad = lw["att3"], lw["rw_pad"]
    p_qkv = _matmul([h_in], lw["w_in"], l, 0, att3, tm_mm, _pick_tile(att3, 512))
    p_rw = _matmul([h_in], lw["w_in"], l, att3, rw_pad, tm_mm, _pick_tile(math.gcd(att3, rw_pad), 512))
    n_cv = lw["w_cv"].shape[2]
    p_cv = _matmul([h_in], lw["w_cv"], 0, 0, n_cv, tm_mm, _pick_tile(n_cv, 512))

    y_att, k_new, v_new = attn_fn(p_qkv)

    r, k, v, a, dec, g = _rw_prep(p_rw, shift0, lw["mu"], lw["w0"], lw["wup"], lw["a0"], lw["aup"],
                                  lw["gup"], tm_seq, tiles_per_seq, rdim, lw["lora_ab"])
    chains, halves, nv = _chain_geometry(nseq, n_rw_heads)
    ck = lambda z: _to_chain_k(z, nseq, t_len, n_rw_heads, halves)
    per_head = lambda z: z.reshape(n_rw_heads, RWKV_HEAD_DIM)
    o_c, s_c = _rw_scan(ck(r), ck(k), ck(a), ck(dec),
                        _to_chain_v(v, nseq, t_len, n_rw_heads, halves, nv),
                        _head_tile_k(per_head(lw["k_k"]), nseq, halves),
                        _head_tile_k(per_head(lw["k_a"]), nseq, halves),
                        _head_tile_k(lw["r_k"], nseq, halves),
                        _head_tile_v(lw["ln_w"], nseq, n_rw_heads, halves, nv),
                        _head_tile_v(lw["ln_b"], nseq, n_rw_heads, halves, nv),
                        _state_to_chain(wkv0, nseq, n_rw_heads, halves, nv),
                        tiles["scan"], halves, chains)
    o_rw = _from_chain_v(o_c, nseq, t_len, n_rw_heads, halves, nv)
    wkv1 = _state_from_chain(s_c, nseq, n_rw_heads, halves, nv)
    shift1 = p_rw.reshape(nseq, t_len, -1)[:, -1, :lw["rw_proj"]]

    y_cv, y_rw, conv1 = _conv_mix(p_cv, conv0, lw["conv_w"], o_rw, g, tm_seq, tiles_per_seq)

    mix = _matmul([y_att, y_rw, y_cv], lw["w_o"], l, 0, d, tm_mm, _pick_tile(d, 512))
    x1, h2 = _resid(mix, x, lw["g_post_mix"], gt_m, tm_ew, tpm, nxt=(lw["g_pre_ffn"], sc_f, sh_f))

    dff = lw["ffn_conv_w"].shape[1]
    f, ffn1 = _ffn_in(h2, lw["w_ffn_in"], l, ffn0, lw["ffn_conv_w"], tiles["ffn"], _pick_tile(dff, 256), t_len)
    fo = _matmul([f], lw["w_ffn_out"], 0, 0, d, tm_mm, _pick_tile(d, 256))
    return x1, fo, gt_f, (k_new, v_new, wkv1, shift1, conv1, ffn1)


def kernel(x_prompt, x_sample, c_prompt, c_sample, cache_k, cache_v, state_wkv, state_shift, state_conv, state_ffn, page_table, w_ada, b_ada, g_pre_mix, g_post_mix, g_pre_ffn, g_post_ffn, w_in, rw_mu, rw_w0, rw_w_up, rw_a0, rw_a_up, rw_g_up, rw_k_k, rw_k_a, rw_r_k, rw_ln_w, rw_ln_b, conv_w, w_o, w_ffn_in, ffn_conv_w, w_ffn_out):
    nb_p, s_len, d = x_prompt.shape
    nb_s, t_s, _ = x_sample.shape
    depth = w_in.shape[0]
    att = cache_k.shape[2] * HEAD_DIM
    n_heads = att // HEAD_DIM
    rdim = rw_w0.shape[1]
    rw_proj = rw_mu.shape[1]
    lora_ab = rw_w_up.shape[1] + rw_a_up.shape[1]
    lora_g = rw_g_up.shape[1]
    lora_g_pad = _round_up(lora_g, LANES)
    rw_pad = _round_up(3 * rdim + lora_ab + lora_g_pad, 512)
    lora_g_pad = rw_pad - 3 * rdim - lora_ab
    assert lora_ab == LANES
    past_len = page_table.shape[1] * PAGE_SIZE
    assert past_len % MOBA_BLOCK == 0 and s_len % MOBA_BLOCK == 0
    n_full = past_len // MOBA_BLOCK
    conv_dim = conv_w.shape[2]
    dff = ffn_conv_w.shape[2]

    xp = x_prompt.reshape(nb_p * s_len, d)
    xs = x_sample.reshape(nb_s * t_s, d)
    n_c = nb_p + nb_s
    c_all = jnp.concatenate([c_prompt, c_sample, jnp.zeros((_round_up(n_c, SUBLANES) - n_c, d), F32)], 0)

    cos_p, sin_p = _rope_tables(jnp.arange(s_len, dtype=jnp.int32))
    cos_s, sin_s = _rope_tables(past_len + jnp.arange(t_s, dtype=jnp.int32))

    tiles_p = dict(mm=_pick_tile(nb_p * s_len, 512), ffn=_pick_tile(s_len, 512), ew=MOBA_BLOCK,
                   seq=MOBA_BLOCK, scan=MOBA_BLOCK // 4, tiles_per_mod=s_len // MOBA_BLOCK)
    tiles_s = dict(mm=nb_s * t_s, ffn=nb_s * t_s, ew=nb_s * t_s, seq=t_s, scan=t_s, tiles_per_mod=1)

    zeros_p = (jnp.zeros((nb_p, 1, rw_pad), F32),
               jnp.zeros((nb_p, rdim // RWKV_HEAD_DIM, RWKV_HEAD_DIM, RWKV_HEAD_DIM), F32),
               jnp.zeros((nb_p, 2, conv_dim), F32), jnp.zeros((nb_p, 2, dff), F32))

    def layer_weights(l):
        pad_rows = lambda w, before, total: jnp.pad(w, ((before, total - before - w.shape[0]), (0, 0)))
        return dict(
            layer=l, att3=3 * att, rw_pad=rw_pad,
            w_in=w_in, w_cv=w_in[l, :, 3 * att + rw_proj:][None],
            w_o=w_o, w_ffn_in=w_ffn_in, w_ffn_out=w_ffn_out[l].astype(BF16)[None],
            mu=jnp.pad(rw_mu[l], (0, rw_pad - rw_proj)), w0=rw_w0[l], a0=rw_a0[l],
            wup=pad_rows(rw_w_up[l], 0, lora_ab).astype(BF16),
            aup=pad_rows(rw_a_up[l], rw_w_up.shape[1], lora_ab).astype(BF16),
            gup=pad_rows(rw_g_up[l], 0, lora_g_pad).astype(BF16),
            k_k=rw_k_k[l], k_a=rw_k_a[l], r_k=rw_r_k[l], ln_w=rw_ln_w[l], ln_b=rw_ln_b[l],
            conv_w=conv_w[l], ffn_conv_w=ffn_conv_w[l],
            g_post_mix=g_post_mix[l], g_pre_ffn=g_pre_ffn[l],
            rdim=rdim, rw_proj=rw_proj, lora_ab=lora_ab)

    def attn_prompt(p_qkv):
        return _moba_prompt(p_qkv, nb_p, s_len, n_heads, cos_p, sin_p)

    def make_attn_sample(l):
        def fn(p_qkv):
            q_s, k_s, v_s = _rope_sample(p_qkv, nb_s, t_s, n_heads, cos_s, sin_s)
            means = _block_means(cache_k, l, page_table, n_full)
            idx = _topk_sample(q_s, means, n_full)
            idx_flat = idx[..., :MOBA_TOPK].reshape(-1)
            y = _attn_sample(q_s, k_s, v_s, cache_k, cache_v, l, idx_flat, page_table)
            return y.astype(BF16), k_s, v_s
        return fn

    outs_p, outs_s = [], []
    hp = hs = None
    fo_p = fo_s = gtf_p = gtf_s = None
    g_post_prev = None
    for l in range(depth):
        ada = _ada(c_all, w_ada, l, b_ada[l])
        mods_p = _mods(ada[:nb_p], d, False, s_len)
        mods_s = _mods(ada[nb_p:n_c], d, True, t_s)
        if l == 0:
            hp = _norm_mod(xp, g_pre_mix[l], mods_p[1], mods_p[0], tiles_p["ew"], tiles_p["tiles_per_mod"])
            hs = _norm_mod(xs, g_pre_mix[l], mods_s[1], mods_s[0], tiles_s["ew"], 1)
        else:
            xp, hp = _resid(fo_p, xp, g_post_prev, gtf_p, tiles_p["ew"], tiles_p["tiles_per_mod"],
                            nxt=(g_pre_mix[l], mods_p[1], mods_p[0]))
            xs, hs = _resid(fo_s, xs, g_post_prev, gtf_s, tiles_s["ew"], 1,
                            nxt=(g_pre_mix[l], mods_s[1], mods_s[0]))
        lw = layer_weights(l)
        xp, fo_p, gtf_p, st_p = _layer(xp, hp, mods_p, nb_p, s_len, lw, attn_prompt, zeros_p, tiles_p)
        states_s = (jnp.pad(state_shift[l], ((0, 0), (0, rw_pad - rw_proj)))[:, None, :], state_wkv[l],
                    state_conv[l], state_ffn[l])
        xs, fo_s, gtf_s, st_s = _layer(xs, hs, mods_s, nb_s, t_s, lw, make_attn_sample(l), states_s, tiles_s)
        g_post_prev = g_post_ffn[l]
        outs_p.append(st_p)
        outs_s.append(st_s)
    xp, _ = _resid(fo_p, xp, g_post_prev, gtf_p, tiles_p["ew"], tiles_p["tiles_per_mod"])
    xs, _ = _resid(fo_s, xs, g_post_prev, gtf_s, tiles_s["ew"], 1)

    stack = lambda sts, i: jnp.stack([st[i] for st in sts])
    return (xp.reshape(nb_p, s_len, d), xs.reshape(nb_s, t_s, d),
            stack(outs_p, 0), stack(outs_p, 1), stack(outs_s, 0), stack(outs_s, 1),
            stack(outs_p, 2), stack(outs_s, 2), stack(outs_p, 3), stack(outs_s, 3),
            stack(outs_p, 4), stack(outs_s, 4), stack(outs_p, 5), stack(outs_s, 5))
```

```python
import functools
import math

import jax
import jax.numpy as jnp
from jax import lax
from jax.experimental import pallas as pl
from jax.experimental.pallas import tpu as pltpu

F32 = jnp.float32
BF16 = jnp.bfloat16

LANES = 128
SUBLANES = 8
VMEM_LIMIT_BYTES = 56 * 1024 * 1024

HEAD_DIM = 128
RWKV_HEAD_DIM = 64
PAGE_SIZE = 128
MOBA_BLOCK = 256
MOBA_TOPK = 3
PAGES_PER_BLOCK = MOBA_BLOCK // PAGE_SIZE
ROPE_THETA = 10000.0
RMS_EPS = 1e-6
GN_EPS = 64e-5
NEG = -1e30
ATT_SCALE = HEAD_DIM ** -0.5

_NT = (((1,), (1,)), ((), ()))


def _cparams(*sem):
    return pltpu.CompilerParams(dimension_semantics=sem, vmem_limit_bytes=VMEM_LIMIT_BYTES)


def _round_up(x, m):
    return (x + m - 1) // m * m


def _pick_tile(n, pref):
    if n <= pref:
        return n
    t = pref
    while t >= LANES:
        if n % t == 0:
            return t
        t -= LANES
    return n


def _sigmoid(x):
    return 1.0 / (1.0 + jnp.exp(-x))


def _split_bf16(x):
    hi = x.astype(BF16)
    lo = (x - hi.astype(F32)).astype(BF16)
    return hi, lo


def _dot3_nt(a, b):
    ah, al = _split_bf16(a)
    bh, bl = _split_bf16(b)
    dg = functools.partial(lax.dot_general, dimension_numbers=_NT, preferred_element_type=F32)
    return dg(ah, bh) + dg(ah, bl) + dg(al, bh)


def _top3_lowest_index(gate, cand, lane):
    sel = jnp.zeros(gate.shape, F32)
    lane = lane.astype(F32)
    firsts = []
    for _ in range(MOBA_TOPK):
        gm = jnp.where(cand > 0.5, gate, -jnp.inf)
        mx = jnp.max(gm, axis=1, keepdims=True)
        is_max = jnp.where(cand > 0.5, jnp.where(gm == mx, 1.0, 0.0), 0.0)
        first = jnp.min(jnp.where(is_max > 0.5, lane, float(LANES)), axis=1, keepdims=True)
        pick = jnp.where(lane == first, 1.0, 0.0)
        sel = sel + pick
        cand = cand - pick
        firsts.append(first)
    return sel, firsts


def _ada_body(c_ref, w_ref, b_ref, o_ref):
    c = c_ref[...]
    s = c * _sigmoid(c)
    o_ref[...] = jnp.dot(s.astype(BF16), w_ref[...].astype(BF16),
                         preferred_element_type=F32) + b_ref[...]


def _ada(c_pad, w_all, layer, b):
    _, d, n = w_all.shape
    tn = _pick_tile(n, 512)
    return pl.pallas_call(
        _ada_body,
        grid=(n // tn,),
        in_specs=[pl.BlockSpec(c_pad.shape, lambda j: (0, 0)),
                  pl.BlockSpec((None, d, tn), lambda j: (layer, 0, j)),
                  pl.BlockSpec((1, tn), lambda j: (0, j))],
        out_specs=pl.BlockSpec((c_pad.shape[0], tn), lambda j: (0, j)),
        out_shape=jax.ShapeDtypeStruct((c_pad.shape[0], n), F32),
        compiler_params=_cparams("parallel"),
        name="ada",
    )(c_pad, w_all, b.reshape(1, n))


def _rms(x, g):
    ms = jnp.mean(x * x, axis=-1, keepdims=True)
    return x * lax.rsqrt(ms + RMS_EPS) * g


def _norm_mod_body(x_ref, g_ref, sc_ref, sh_ref, h_ref):
    h_ref[...] = (_rms(x_ref[...], g_ref[...]) * sc_ref[...] + sh_ref[...]).astype(h_ref.dtype)


def _mod_spec(mod, tiles_per_mod):
    return pl.BlockSpec((None,) + mod.shape[1:], lambda i: (i // tiles_per_mod, 0, 0))


def _norm_mod(x, g, sc1p, sh, tm, tiles_per_mod):
    m, d = x.shape
    row = pl.BlockSpec((tm, d), lambda i: (i, 0))
    return pl.pallas_call(
        _norm_mod_body,
        grid=(m // tm,),
        in_specs=[row, pl.BlockSpec((1, d), lambda i: (0, 0)),
                  _mod_spec(sc1p, tiles_per_mod), _mod_spec(sh, tiles_per_mod)],
        out_specs=row,
        out_shape=jax.ShapeDtypeStruct((m, d), BF16),
        compiler_params=_cparams("parallel"),
        name="norm_mod",
    )(x, g.reshape(1, d), sc1p, sh)


def _resid_body(*refs, with_h):
    if with_h:
        y_ref, x_ref, gp_ref, gate_ref, gn_ref, sc_ref, sh_ref, xo_ref, h_ref = refs
    else:
        y_ref, x_ref, gp_ref, gate_ref, xo_ref = refs
    xn = x_ref[...] + gate_ref[...] * _rms(y_ref[...], gp_ref[...])
    xo_ref[...] = xn
    if with_h:
        h_ref[...] = (_rms(xn, gn_ref[...]) * sc_ref[...] + sh_ref[...]).astype(h_ref.dtype)


def _resid(y, x, g_post, gate1p, tm, tiles_per_mod, nxt=None):
    m, d = x.shape
    row = pl.BlockSpec((tm, d), lambda i: (i, 0))
    vec = pl.BlockSpec((1, d), lambda i: (0, 0))
    ins = [y, x, g_post.reshape(1, d), gate1p]
    specs = [row, row, vec, _mod_spec(gate1p, tiles_per_mod)]
    out_shape = [jax.ShapeDtypeStruct((m, d), F32)]
    out_specs = [row]
    if nxt is not None:
        g_next, sc1p, sh = nxt
        ins += [g_next.reshape(1, d), sc1p, sh]
        specs += [vec, _mod_spec(sc1p, tiles_per_mod), _mod_spec(sh, tiles_per_mod)]
        out_shape.append(jax.ShapeDtypeStruct((m, d), BF16))
        out_specs.append(row)
    res = pl.pallas_call(
        functools.partial(_resid_body, with_h=nxt is not None),
        grid=(m // tm,),
        in_specs=specs,
        out_specs=out_specs,
        out_shape=out_shape,
        compiler_params=_cparams("parallel"),
        name="resid",
    )(*ins)
    return res if nxt is not None else (res[0], None)


def _mm_body(*refs, k_sizes, cast_w, w_is_t):
    n_lhs = len(k_sizes)
    w_ref, o_ref = refs[n_lhs], refs[n_lhs + 1]
    if cast_w:
        wbf = refs[n_lhs + 2]

        @pl.when(pl.program_id(1) == 0)
        def _():
            w = w_ref[...]
            wbf[...] = (w.T if w_is_t else w).astype(BF16)

        w_ref = wbf
    acc = None
    off = 0
    for a_ref, ks in zip(refs[:n_lhs], k_sizes):
        part = jnp.dot(a_ref[...], w_ref[off:off + ks, :], preferred_element_type=F32)
        acc = part if acc is None else acc + part
        off += ks
    o_ref[...] = acc.astype(o_ref.dtype)


def _matmul(lhs_list, w_all, layer, col0, n, tm, tn, w_is_t=False):
    m = lhs_list[0].shape[0]
    k_sizes = tuple(a.shape[1] for a in lhs_list)
    k = w_all.shape[2 if w_is_t else 1]
    assert sum(k_sizes) == k and m % tm == 0 and n % tn == 0 and col0 % tn == 0
    assert col0 + n <= w_all.shape[1 if w_is_t else 2]
    cast_w = w_all.dtype != BF16
    assert cast_w or not w_is_t
    j0 = col0 // tn
    w_spec = (pl.BlockSpec((None, tn, k), lambda j, i: (layer, j0 + j, 0)) if w_is_t
              else pl.BlockSpec((None, k, tn), lambda j, i: (layer, 0, j0 + j)))
    return pl.pallas_call(
        functools.partial(_mm_body, k_sizes=k_sizes, cast_w=cast_w, w_is_t=w_is_t),
        grid=(n // tn, m // tm),
        in_specs=[pl.BlockSpec((tm, ks), lambda j, i: (i, 0)) for ks in k_sizes] + [w_spec],
        out_specs=pl.BlockSpec((tm, tn), lambda j, i: (i, j)),
        out_shape=jax.ShapeDtypeStruct((m, n), F32),
        scratch_shapes=[pltpu.VMEM((k, tn), BF16)] if cast_w else [],
        compiler_params=_cparams("parallel", "arbitrary"),
        name="matmul",
    )(*lhs_list, w_all)


def _shifted_rows(u, halo, state, first):
    p1 = jnp.where(first, state[1:2, :], halo[SUBLANES - 1:SUBLANES, :])
    p2 = jnp.where(first, state[0:1, :], halo[SUBLANES - 2:SUBLANES - 1, :])
    row = lax.broadcasted_iota(jnp.int32, u.shape, 0)
    u1 = jnp.where(row == 0, p1, pltpu.roll(u, 1, 0))
    u2 = jnp.where(row == 0, p2, jnp.where(row == 1, p1, pltpu.roll(u, 2, 0)))
    return u1, u2


def _shifted_rows_short(u, prev1, prev2, t_len):
    pos = lax.broadcasted_iota(jnp.int32, u.shape, 0) % t_len
    u1 = jnp.where(pos == 0, prev1, pltpu.roll(u, 1, 0))
    u2 = jnp.where(pos == 0, prev2, jnp.where(pos == 1, prev1, pltpu.roll(u, 2, 0)))
    return u1, u2


def _ffn_in_body(*refs, tiles_per_seq, t_len):
    short = tiles_per_seq == 0
    if short:
        h_ref, wg_ref, wu_ref, p1_ref, p2_ref, cw_ref, f_ref, u_ref, wgb, wub = refs
    else:
        h_ref, wg_ref, wu_ref, s0_ref, cw_ref, f_ref, s1_ref, wgb, wub, tail = refs
    i = pl.program_id(1)

    @pl.when(i == 0)
    def _():
        wgb[...] = wg_ref[...].astype(BF16)
        wub[...] = wu_ref[...].astype(BF16)
        if not short:
            tail[...] = jnp.zeros(tail.shape, F32)

    h = h_ref[...]
    u = jnp.dot(h, wgb[...], preferred_element_type=F32)
    up = jnp.dot(h, wub[...], preferred_element_type=F32)
    if short:
        u1, u2 = _shifted_rows_short(u, p1_ref[...], p2_ref[...], t_len)
        u_ref[...] = u
    else:
        u1, u2 = _shifted_rows(u, tail[...], s0_ref[...], (i % tiles_per_seq) == 0)
        tail[...] = u[u.shape[0] - SUBLANES:, :]

        @pl.when((i % tiles_per_seq) == tiles_per_seq - 1)
        def _():
            s1_ref[...] = u[u.shape[0] - 2:, :]

    w = cw_ref[...]
    y = u2 * w[0:1, :] + u1 * w[1:2, :] + u * w[2:3, :]
    f_ref[...] = (y * _sigmoid(y) * up).astype(f_ref.dtype)


def _ffn_in(h, w_all, layer, ffn0, ffn_w, tm, tc, t_len):
    m, k = h.shape
    dff = ffn_w.shape[1]
    nseq = ffn0.shape[0]
    ncol = dff // tc
    assert dff % tc == 0 and m % tm == 0
    short = tm > t_len
    tiles_per_seq = 0 if short else t_len // tm
    lhs = pl.BlockSpec((tm, k), lambda j, i: (i, 0))
    wspec = lambda off: pl.BlockSpec((None, k, tc), lambda j, i: (layer, 0, off + j))
    tile = pl.BlockSpec((tm, tc), lambda j, i: (i, j))
    cw = pl.BlockSpec((3, tc), lambda j, i: (0, j))
    scratch = [pltpu.VMEM((k, tc), BF16), pltpu.VMEM((k, tc), BF16)]
    if short:
        assert tm == m and m == nseq * t_len
        prev1 = jnp.repeat(ffn0[:, 1, :], t_len, axis=0)
        prev2 = jnp.repeat(ffn0[:, 0, :], t_len, axis=0)
        f, u = pl.pallas_call(
            functools.partial(_ffn_in_body, tiles_per_seq=0, t_len=t_len),
            grid=(ncol, 1),
            in_specs=[lhs, wspec(0), wspec(ncol), tile, tile, cw],
            out_specs=[tile, tile],
            out_shape=[jax.ShapeDtypeStruct((m, dff), BF16), jax.ShapeDtypeStruct((m, dff), F32)],
            scratch_shapes=scratch,
            compiler_params=_cparams("parallel", "arbitrary"),
            name="ffn_in",
        )(h, w_all, w_all, prev1, prev2, ffn_w)
        return f, u.reshape(nseq, t_len, dff)[:, t_len - 2:, :]
    state = pl.BlockSpec((None, 2, tc), lambda j, i: (i // tiles_per_seq, 0, j))
    return pl.pallas_call(
        functools.partial(_ffn_in_body, tiles_per_seq=tiles_per_seq, t_len=t_len),
        grid=(ncol, m // tm),
        in_specs=[lhs, wspec(0), wspec(ncol), state, cw],
        out_specs=[tile, state],
        out_shape=[jax.ShapeDtypeStruct((m, dff), BF16), jax.ShapeDtypeStruct((nseq, 2, dff), F32)],
        scratch_shapes=scratch + [pltpu.VMEM((SUBLANES, tc), F32)],
        compiler_params=_cparams("parallel", "arbitrary"),
        name="ffn_in",
    )(h, w_all, w_all, ffn0, ffn_w)


def _rope_tables(pos):
    half = HEAD_DIM // 2
    inv = ROPE_THETA ** (-jnp.arange(half, dtype=F32) / half)
    ang = pos.astype(F32)[:, None] * inv[None, :]
    cos, sin = jnp.cos(ang), jnp.sin(ang)
    return jnp.concatenate([cos, cos], axis=1), jnp.concatenate([-sin, sin], axis=1)


def _rope(x, cos2, sin2):
    return x * cos2 + pltpu.roll(x, HEAD_DIM // 2, 1) * sin2


def _moba_p_body(q_ref, k_ref, v_ref, cos_ref, sin_ref, e_ref, y_ref, ko_ref, vo_ref, kext, vbf, means):
    s_len = k_ref.shape[0]
    nb = s_len // MOBA_BLOCK
    n_pages = s_len // PAGE_SIZE

    kr = _rope(k_ref[...], cos_ref[...], sin_ref[...])
    v = v_ref[...]
    ko_ref[...] = kr.reshape(n_pages, PAGE_SIZE, HEAD_DIM)
    vo_ref[...] = v.reshape(n_pages, PAGE_SIZE, HEAD_DIM)
    kext[:, 0:HEAD_DIM] = kr.astype(BF16)
    kext[:, HEAD_DIM:] = e_ref[...]
    vbf[...] = v.astype(BF16)
    means[...] = jnp.zeros(means.shape, F32)
    means[0:nb, :] = jnp.mean(kr.reshape(nb, MOBA_BLOCK, HEAD_DIM), axis=1)

    lane = lax.broadcasted_iota(jnp.int32, (MOBA_BLOCK, LANES), 1)
    row = lax.broadcasted_iota(jnp.int32, (MOBA_BLOCK, MOBA_BLOCK), 0)
    col = lax.broadcasted_iota(jnp.int32, (MOBA_BLOCK, MOBA_BLOCK), 1)
    for n in range(nb):
        rows = slice(n * MOBA_BLOCK, (n + 1) * MOBA_BLOCK)
        qr = _rope(q_ref[rows, :], cos_ref[rows, :], sin_ref[rows, :])
        keep = jnp.where(lane == n, 1.0, 0.0)
        if n > 0:
            gate = _dot3_nt(qr, means[...])
            cand = jnp.where(lane < n, 1.0, 0.0)
            sel, _ = _top3_lowest_index(gate, cand, lane)
            keep = keep + sel
        qext = jnp.concatenate([qr.astype(BF16), (keep - 1.0).astype(BF16)], axis=1)
        end = (n + 1) * MOBA_BLOCK
        s = lax.dot_general(qext, kext[0:end, :], _NT, preferred_element_type=F32) * ATT_SCALE
        s_own = jnp.where(col <= row, s[:, n * MOBA_BLOCK:end], NEG)
        mx = jnp.max(s_own, axis=1, keepdims=True)
        if n > 0:
            s_past = s[:, 0:n * MOBA_BLOCK]
            mx = jnp.maximum(mx, jnp.max(s_past, axis=1, keepdims=True))
        p_own = jnp.exp(s_own - mx)
        l = jnp.sum(p_own, axis=1, keepdims=True)
        o = jnp.dot(p_own.astype(BF16), vbf[n * MOBA_BLOCK:end, :], preferred_element_type=F32)
        if n > 0:
            p_past = jnp.exp(s_past - mx)
            l = l + jnp.sum(p_past, axis=1, keepdims=True)
            o = o + jnp.dot(p_past.astype(BF16), vbf[0:n * MOBA_BLOCK, :], preferred_element_type=F32)
        y_ref[rows, :] = (o / l).astype(y_ref.dtype)


MASK_BIG = 1e30


def _moba_prompt(p_qkv, nseq, s_len, n_heads, cos2, sin2):
    n_pages = s_len // PAGE_SIZE
    m = nseq * s_len
    e = jnp.where(jnp.arange(s_len)[:, None] // MOBA_BLOCK == jnp.arange(LANES)[None, :],
                  MASK_BIG, 0.0).astype(BF16)
    full = lambda off: pl.BlockSpec((s_len, HEAD_DIM), lambda b, h: (b, off + h))
    tab = pl.BlockSpec((s_len, HEAD_DIM), lambda b, h: (0, 0))
    pages = pl.BlockSpec((None, n_pages, None, PAGE_SIZE, HEAD_DIM), lambda b, h: (b, 0, h, 0, 0))
    page_shape = jax.ShapeDtypeStruct((nseq, n_pages, n_heads, PAGE_SIZE, HEAD_DIM), F32)
    return pl.pallas_call(
        _moba_p_body,
        grid=(nseq, n_heads),
        in_specs=[full(0), full(n_heads), full(2 * n_heads), tab, tab, tab],
        out_specs=[full(0), pages, pages],
        out_shape=[jax.ShapeDtypeStruct((m, n_heads * HEAD_DIM), BF16), page_shape, page_shape],
        scratch_shapes=[pltpu.VMEM((s_len, HEAD_DIM + LANES), BF16), pltpu.VMEM((s_len, HEAD_DIM), BF16),
                        pltpu.VMEM((LANES, HEAD_DIM), F32)],
        compiler_params=_cparams("parallel", "parallel"),
        name="moba_prompt",
    )(p_qkv, p_qkv, p_qkv, cos2, sin2, e)


def _rope_s_body(p_ref, cos_ref, sin_ref, q_o, k_o, v_o, *, n_heads):
    cos2, sin2 = cos_ref[...], sin_ref[...]
    for h in range(n_heads):
        sl = lambda part: p_ref[:, (part * n_heads + h) * HEAD_DIM:(part * n_heads + h + 1) * HEAD_DIM]
        q_o[h] = _rope(sl(0), cos2, sin2)
        k_o[h] = _rope(sl(1), cos2, sin2)
        v_o[h] = sl(2)


def _rope_sample(p_qkv, nseq, t_len, n_heads, cos2, sin2):
    shp = jax.ShapeDtypeStruct((nseq, n_heads, t_len, HEAD_DIM), F32)
    out = pl.BlockSpec((None, n_heads, t_len, HEAD_DIM), lambda b: (b, 0, 0, 0))
    tab = pl.BlockSpec((t_len, HEAD_DIM), lambda b: (0, 0))
    return pl.pallas_call(
        functools.partial(_rope_s_body, n_heads=n_heads),
        grid=(nseq,),
        in_specs=[pl.BlockSpec((t_len, p_qkv.shape[1]), lambda b: (b, 0)), tab, tab],
        out_specs=[out, out, out],
        out_shape=[shp, shp, shp],
        compiler_params=_cparams("parallel"),
        name="rope_sample",
    )(p_qkv, cos2, sin2)


_MEANS_BLOCKS_PER_STEP = 4


def _means_body(pt_ref, *refs, n_heads):
    page_refs, o_ref = refs[:-1], refs[-1]
    j = pl.program_id(1)

    @pl.when(j == 0)
    def _():
        o_ref[...] = jnp.zeros(o_ref.shape, F32)

    for blk in range(_MEANS_BLOCKS_PER_STEP):
        s = jnp.sum(page_refs[PAGES_PER_BLOCK * blk][...], axis=1)
        for pg in range(1, PAGES_PER_BLOCK):
            s = s + jnp.sum(page_refs[PAGES_PER_BLOCK * blk + pg][...], axis=1)
        s = s * (1.0 / MOBA_BLOCK)
        for h in range(n_heads):
            o_ref[h, pl.ds(j * _MEANS_BLOCKS_PER_STEP + blk, 1), :] = s[h:h + 1, :]


def _block_means(cache_k, layer, page_table, n_full):
    nseq = page_table.shape[0]
    n_heads = cache_k.shape[2]
    pages_per_step = PAGES_PER_BLOCK * _MEANS_BLOCKS_PER_STEP
    assert n_full <= LANES and n_full % _MEANS_BLOCKS_PER_STEP == 0
    page = lambda pg: pl.BlockSpec(
        (None, None, n_heads, PAGE_SIZE, HEAD_DIM),
        lambda b, j, pt: (layer, pt[b, pages_per_step * j + pg], 0, 0, 0))
    return pl.pallas_call(
        functools.partial(_means_body, n_heads=n_heads),
        grid_spec=pltpu.PrefetchScalarGridSpec(
            num_scalar_prefetch=1,
            grid=(nseq, n_full // _MEANS_BLOCKS_PER_STEP),
            in_specs=[page(pg) for pg in range(pages_per_step)],
            out_specs=pl.BlockSpec((None, n_heads, LANES, HEAD_DIM), lambda b, j, pt: (b, 0, 0, 0)),
        ),
        out_shape=jax.ShapeDtypeStruct((nseq, n_heads, LANES, HEAD_DIM), F32),
        compiler_params=_cparams("parallel", "arbitrary"),
        name="block_means",
    )(page_table, *([cache_k] * pages_per_step))


def _topk_s_body(q_ref, m_ref, o_ref, *, n_heads, n_full):
    for h in range(n_heads):
        gate = _dot3_nt(q_ref[h], m_ref[h])
        lane = lax.broadcasted_iota(jnp.int32, gate.shape, 1)
        cand = jnp.where(lane < n_full, 1.0, 0.0)
        _, firsts = _top3_lowest_index(gate, cand, lane)
        idx = jnp.zeros(gate.shape, F32)
        for r, first in enumerate(firsts):
            idx = jnp.where(lane == r, first, idx)
        o_ref[h] = idx.astype(jnp.int32)


def _topk_sample(q_s, means, n_full):
    nseq, n_heads, t_len, _ = q_s.shape
    return pl.pallas_call(
        functools.partial(_topk_s_body, n_heads=n_heads, n_full=n_full),
        grid=(nseq,),
        in_specs=[pl.BlockSpec((None, n_heads, t_len, HEAD_DIM), lambda b: (b, 0, 0, 0)),
                  pl.BlockSpec((None, n_heads, LANES, HEAD_DIM), lambda b: (b, 0, 0, 0))],
        out_specs=pl.BlockSpec((None, n_heads, t_len, LANES), lambda b: (b, 0, 0, 0)),
        out_shape=jax.ShapeDtypeStruct((nseq, n_heads, t_len, LANES), jnp.int32),
        compiler_params=_cparams("parallel"),
        name="topk_sample",
    )(q_s, means)


_N_SEL_PAGES = MOBA_TOPK * PAGES_PER_BLOCK


def _attn_s_body(idx_ref, pt_ref, q_ref, kn_ref, vn_ref, *refs):
    q = q_ref[...]
    t_len = q.shape[0]
    n_pg = t_len * _N_SEL_PAGES
    k_refs, v_refs, o_ref = refs[:n_pg], refs[n_pg:2 * n_pg], refs[-1]
    kn, vn = kn_ref[...], vn_ref[...]
    key_new = lax.broadcasted_iota(jnp.int32, (t_len, 1), 0)
    outs = []
    for t in range(t_len):
        q_t = q[t:t + 1, :]
        pages = slice(t * _N_SEL_PAGES, (t + 1) * _N_SEL_PAGES)
        score = lambda keys: jnp.sum(keys * q_t, axis=1, keepdims=True) * ATT_SCALE
        s_all = [score(k[...]) for k in k_refs[pages]]
        s_all.append(jnp.where(key_new <= t, score(kn), NEG))
        mx = jnp.max(s_all[0], axis=0, keepdims=True)
        for s in s_all[1:]:
            mx = jnp.maximum(mx, jnp.max(s, axis=0, keepdims=True))
        l = jnp.zeros((1, 1), F32)
        o = jnp.zeros((1, HEAD_DIM), F32)
        for s, v in zip(s_all, [v[...] for v in v_refs[pages]] + [vn]):
            p = jnp.exp(s - mx)
            l = l + jnp.sum(p, axis=0, keepdims=True)
            o = o + jnp.sum(p * v, axis=0, keepdims=True)
        outs.append(o / l)
    o_ref[...] = jnp.concatenate(outs, axis=0)


def _attn_sample(q_s, k_new, v_new, cache_k, cache_v, layer, idx_flat, page_table):
    nseq, n_heads, t_len, _ = q_s.shape

    def page(t, s, pg):
        def imap(b, h, idx, pt):
            blk = idx[((b * n_heads + h) * t_len + t) * MOBA_TOPK + s]
            return (layer, pt[b, blk * PAGES_PER_BLOCK + pg], h, 0, 0)
        return pl.BlockSpec((None, None, None, PAGE_SIZE, HEAD_DIM), imap)

    pages = [page(t, s, pg) for t in range(t_len) for s in range(MOBA_TOPK) for pg in range(PAGES_PER_BLOCK)]
    new = pl.BlockSpec((None, None, t_len, HEAD_DIM), lambda b, h, idx, pt: (b, h, 0, 0))
    return pl.pallas_call(
        _attn_s_body,
        grid_spec=pltpu.PrefetchScalarGridSpec(
            num_scalar_prefetch=2,
            grid=(nseq, n_heads),
            in_specs=[new, new, new] + pages + pages,
            out_specs=pl.BlockSpec((t_len, HEAD_DIM), lambda b, h, idx, pt: (b, h)),
        ),
        out_shape=jax.ShapeDtypeStruct((nseq * t_len, n_heads * HEAD_DIM), F32),
        compiler_params=_cparams("parallel", "parallel"),
        name="attn_sample",
    )(idx_flat, page_table, q_s, k_new, v_new, *([cache_k] * len(pages)), *([cache_v] * len(pages)))


def _rw_prep_body(p_ref, halo_ref, sp_ref, mu_ref, w0_ref, wup_ref, a0_ref, aup_ref, gup_ref,
                  r_o, k_o, v_o, a_o, d_o, g_o, *, tiles_per_seq, rdim, lora_ab):
    i = pl.program_id(0)
    pf = p_ref[...]
    first = (i % tiles_per_seq) == 0
    prev0 = jnp.where(first, sp_ref[...], halo_ref[SUBLANES - 1:SUBLANES, :])
    row = lax.broadcasted_iota(jnp.int32, pf.shape, 0)
    prev = jnp.where(row == 0, prev0, pltpu.roll(pf, 1, 0))
    pm = pf + (prev - pf) * mu_ref[...]
    r, k, v = pm[:, 0:rdim], pm[:, rdim:2 * rdim], pm[:, 2 * rdim:3 * rdim]
    ca = pm[:, 3 * rdim:3 * rdim + lora_ab]
    cb = pm[:, 3 * rdim + lora_ab:]
    w_lin = jnp.dot(jnp.tanh(ca).astype(BF16), wup_ref[...], preferred_element_type=F32)
    a_lin = jnp.dot(ca.astype(BF16), aup_ref[...], preferred_element_type=F32)
    g = jnp.dot(_sigmoid(cb).astype(BF16), gup_ref[...], preferred_element_type=F32)
    y = -(w0_ref[...] + w_lin)
    w = -(jnp.maximum(y, 0.0) + jnp.log(1.0 + jnp.exp(-jnp.abs(y)))) - 0.5
    a = _sigmoid(a0_ref[...] + a_lin)
    r_o[...] = r
    k_o[...] = k
    v_o[...] = v
    a_o[...] = a
    d_o[...] = jnp.exp(-jnp.exp(w))
    g_o[...] = g


def _rw_prep(p_rw, shift_prev, mu, w0, wup, a0, aup, gup, tm, tiles_per_seq, rdim, lora_ab):
    m, wp = p_rw.shape
    row = pl.BlockSpec((tm, wp), lambda i: (i, 0))
    halo = pl.BlockSpec((SUBLANES, wp), lambda i: (jnp.maximum(i * (tm // SUBLANES) - 1, 0), 0))
    const = lambda a: pl.BlockSpec(a.shape, lambda i: (0,) * a.ndim)
    out = pl.BlockSpec((tm, rdim), lambda i: (i, 0))
    shp = jax.ShapeDtypeStruct((m, rdim), F32)
    vecs = [x.reshape(1, -1) for x in (mu, w0)]
    return pl.pallas_call(
        functools.partial(_rw_prep_body, tiles_per_seq=tiles_per_seq, rdim=rdim, lora_ab=lora_ab),
        grid=(m // tm,),
        in_specs=[row, halo, pl.BlockSpec((None, 1, wp), lambda i: (i // tiles_per_seq, 0, 0)),
                  const(vecs[0]), const(vecs[1]), const(wup), const(a0.reshape(1, -1)), const(aup),
                  const(gup)],
        out_specs=[out] * 6,
        out_shape=[shp] * 6,
        compiler_params=_cparams("parallel"),
        name="rwkv_prep",
    )(p_rw, p_rw, shift_prev, vecs[0], vecs[1], wup, a0.reshape(1, -1), aup, gup)


_SCAN_ROWS_IN_FLIGHT = 4


def _rw_scan_body(r_ref, k_ref, a_ref, d_ref, v_ref, kkt_ref, kat_ref, rk_ref, lnw_ref, lnb_ref, s0_ref,
                  o_ref, so_ref, state, b_s, c_s, kp_s, d_s, r_s, bd_s, al_s, be_s, rk_s, o_s,
                  *, halves, chains, packed):
    pid = pl.program_id(0)
    tc, nv = v_ref.shape[0], v_ref.shape[1]
    n_val = nv * halves
    rows = min(_SCAN_ROWS_IN_FLIGHT, nv)

    @pl.when(pid == 0)
    def _():
        state[...] = s0_ref[...]

    np_ = tc // 2

    def even_odd(ref):
        x = ref[...]
        if packed:
            xr = pltpu.roll(x.reshape(np_ * x.shape[1], LANES), LANES // 2, 1).reshape(x.shape)
            low = lax.broadcasted_iota(jnp.int32, x.shape, 2) < LANES // 2
            return jnp.where(low, x, xr), jnp.where(low, xr, x)
        x = x.reshape((np_, 2) + x.shape[1:])
        return x[:, 0], x[:, 1]

    k2, a2, d2, r2 = even_odd(k_ref), even_odd(a_ref), even_odd(d_ref), even_odd(r_ref)
    b2, c2, kp2 = [], [], []
    for par in range(2):
        k, a = k2[par], a2[par]
        kk = k * kkt_ref[...]
        kkn = kk / jnp.maximum(jnp.sqrt(jnp.sum(kk * kk, axis=1, keepdims=True)), 1e-12)
        kp = k * (1.0 + (a - 1.0) * kat_ref[...])
        b2.append(-kkn)
        c2.append(kkn * a)
        kp2.append(kp)
        b_s[par] = b2[par]
        c_s[par] = c2[par]
        kp_s[par] = kp
        d_s[par] = d2[par]
        r_s[par] = r2[par]
        rk_s[par] = jnp.sum(r2[par] * kp * rk_ref[...], axis=1, keepdims=True)
    bd_s[...] = d2[0] * b2[1]
    al_s[...] = jnp.sum(c2[0] * b2[1], axis=1, keepdims=True)
    be_s[...] = jnp.sum(kp2[0] * b2[1], axis=1, keepdims=True)

    for g in range(nv // rows):
        def step(s, s_rows, g=g):
            t0, t1 = 2 * s, 2 * s + 1
            b0, bd, al, be = b_s[0, s], bd_s[s], al_s[s], be_s[s]
            c0, kp0, d0, r0 = c_s[0, s], kp_s[0, s], d_s[0, s], r_s[0, s]
            c1, kp1, d1, r1 = c_s[1, s], kp_s[1, s], d_s[1, s], r_s[1, s]
            new = []
            for i, sv in enumerate(s_rows):
                vi = g * rows + i
                v0, v1 = v_ref[t0, vi:vi + 1, :], v_ref[t1, vi:vi + 1, :]
                u0 = jnp.sum(sv * b0, axis=0, keepdims=True)
                u1 = jnp.sum(sv * bd, axis=0, keepdims=True) + u0 * al + v0 * be
                s1 = sv * d0 + u0 * c0 + v0 * kp0
                o_s[t0, vi:vi + 1, :] = jnp.sum(s1 * r0, axis=0, keepdims=True)
                s2 = s1 * d1 + u1 * c1 + v1 * kp1
                o_s[t1, vi:vi + 1, :] = jnp.sum(s2 * r1, axis=0, keepdims=True)
                new.append(s2)
            return tuple(new)

        s_rows = lax.fori_loop(0, tc // 2, step, tuple(state[g * rows + i] for i in range(rows)))
        for i, sv in enumerate(s_rows):
            state[g * rows + i] = sv

    def head_sum(x):
        x = x.reshape(tc * nv, LANES)
        shift = chains
        for _ in range(int(math.log2(halves))):
            x = x + pltpu.roll(x, shift, 1)
            shift *= 2
        return jnp.sum(x.reshape(tc, nv, LANES), axis=1, keepdims=True)

    o = o_s[...]
    dev = o - head_sum(o) * (1.0 / n_val)
    var = head_sum(dev * dev) * (1.0 / n_val)
    on = dev * lax.rsqrt(var + GN_EPS) * lnw_ref[...] + lnb_ref[...]
    rk = jnp.stack([rk_s[0], rk_s[1]], axis=1).reshape(tc, 1, LANES)
    o_ref[...] = on + rk * v_ref[...]

    @pl.when(pid == pl.num_programs(0) - 1)
    def _():
        so_ref[...] = state[...]


def _rw_scan(r, k, a, d, v, kk_tile, ka_tile, rk_tile, lnw_tile, lnb_tile, s0, tc, halves, chains, packed):
    t_len, nv = v.shape[0], v.shape[1]
    kdim = r.shape[1]
    assert nv % min(_SCAN_ROWS_IN_FLIGHT, nv) == 0 and t_len % tc == 0 and tc % 2 == 0
    np_ = tc // 2
    ktile = pl.BlockSpec((np_ if packed else tc, kdim, LANES), lambda i: (i, 0, 0))
    vtile = pl.BlockSpec((tc, nv, LANES), lambda i: (i, 0, 0))
    const = lambda x: pl.BlockSpec(x.shape, lambda i: (0,) * x.ndim)
    consts = [kk_tile, ka_tile, rk_tile, lnw_tile, lnb_tile, s0]
    return pl.pallas_call(
        functools.partial(_rw_scan_body, halves=halves, chains=chains, packed=packed),
        grid=(t_len // tc,),
        in_specs=[ktile] * 4 + [vtile] + [const(x) for x in consts],
        out_specs=[vtile, const(s0)],
        out_shape=[jax.ShapeDtypeStruct(v.shape, F32), jax.ShapeDtypeStruct(s0.shape, F32)],
        scratch_shapes=[pltpu.VMEM(s0.shape, F32)] + [pltpu.VMEM((2, np_, kdim, LANES), F32)] * 5
        + [pltpu.VMEM((np_, kdim, LANES), F32)] + [pltpu.VMEM((np_, 1, LANES), F32)] * 2
        + [pltpu.VMEM((2, np_, 1, LANES), F32), pltpu.VMEM((tc, nv, LANES), F32)],
        compiler_params=_cparams("arbitrary"),
        name="rwkv_scan",
    )(r, k, a, d, v, *consts)


def _conv_mix_body(gb_ref, gc_ref, hc_ref, gch_ref, hch_ref, c0_ref, w_ref, o_ref, g_ref,
                   ycv_ref, yrw_ref, c1_ref, *, tiles_per_seq):
    i = pl.program_id(0)
    first = (i % tiles_per_seq) == 0
    u = gc_ref[...] * hc_ref[...]
    u1, u2 = _shifted_rows(u, gch_ref[...] * hch_ref[...], c0_ref[...], first)
    w = w_ref[...]
    y = u2 * w[0:1, :] + u1 * w[1:2, :] + u * w[2:3, :]
    ycv_ref[...] = (gb_ref[...] * y).astype(ycv_ref.dtype)
    yrw_ref[...] = (o_ref[...] * g_ref[...]).astype(yrw_ref.dtype)

    @pl.when((i % tiles_per_seq) == tiles_per_seq - 1)
    def _():
        c1_ref[...] = u[u.shape[0] - 2:, :]


def _conv_mix(p_cv, conv0, conv_w, o_rw, g_rw, tm, tiles_per_seq):
    m = p_cv.shape[0]
    c = conv_w.shape[1]
    nseq = conv0.shape[0]
    col = lambda j: pl.BlockSpec((tm, c), lambda i: (i, j))
    halo = lambda j: pl.BlockSpec((SUBLANES, c), lambda i: (jnp.maximum(i * (tm // SUBLANES) - 1, 0), j))
    state = pl.BlockSpec((None, 2, c), lambda i: (i // tiles_per_seq, 0, 0))
    rw = pl.BlockSpec((tm, o_rw.shape[1]), lambda i: (i, 0))
    return pl.pallas_call(
        functools.partial(_conv_mix_body, tiles_per_seq=tiles_per_seq),
        grid=(m // tm,),
        in_specs=[col(0), col(1), col(2), halo(1), halo(2), state,
                  pl.BlockSpec((3, c), lambda i: (0, 0)), rw, rw],
        out_specs=[col(0), rw, state],
        out_shape=[jax.ShapeDtypeStruct((m, c), BF16), jax.ShapeDtypeStruct(o_rw.shape, BF16),
                   jax.ShapeDtypeStruct((nseq, 2, c), F32)],
        compiler_params=_cparams("arbitrary"),
        name="conv_mix",
    )(p_cv, p_cv, p_cv, p_cv, p_cv, conv0, conv_w, o_rw, g_rw)


def _chain_geometry(nseq, n_heads):
    chains = nseq * n_heads
    assert LANES % chains == 0
    halves = LANES // chains
    assert RWKV_HEAD_DIM % halves == 0
    return chains, halves, RWKV_HEAD_DIM // halves


def _to_chain_k(x, nseq, t_len, n_heads, halves):
    if halves == 2:
        x = x.reshape(nseq, t_len // 2, 2, n_heads, RWKV_HEAD_DIM).transpose(1, 4, 2, 0, 3)
        return x.reshape(t_len // 2, RWKV_HEAD_DIM, LANES)
    x = x.reshape(nseq, t_len, n_heads, RWKV_HEAD_DIM).transpose(1, 3, 0, 2)
    return jnp.tile(x.reshape(t_len, RWKV_HEAD_DIM, nseq * n_heads), (1, 1, halves))


def _to_chain_v(x, nseq, t_len, n_heads, halves, nv):
    x = x.reshape(nseq, t_len, n_heads, halves, nv).transpose(1, 4, 3, 0, 2)
    return x.reshape(t_len, nv, LANES)


def _from_chain_v(x, nseq, t_len, n_heads, halves, nv):
    x = x.reshape(t_len, nv, halves, nseq, n_heads).transpose(3, 0, 4, 2, 1)
    return x.reshape(nseq * t_len, n_heads * RWKV_HEAD_DIM)


def _state_to_chain(s, nseq, n_heads, halves, nv):
    s = s.reshape(nseq, n_heads, halves, nv, RWKV_HEAD_DIM).transpose(3, 4, 2, 0, 1)
    return s.reshape(nv, RWKV_HEAD_DIM, LANES)


def _state_from_chain(s, nseq, n_heads, halves, nv):
    s = s.reshape(nv, RWKV_HEAD_DIM, halves, nseq, n_heads).transpose(3, 4, 2, 0, 1)
    return s.reshape(nseq, n_heads, RWKV_HEAD_DIM, RWKV_HEAD_DIM)


def _head_tile_k(x, nseq, halves):
    return jnp.tile(x.T, (1, halves * nseq))


def _head_tile_v(x, nseq, n_heads, halves, nv):
    x = x.reshape(n_heads, halves, nv).transpose(2, 1, 0)
    return jnp.broadcast_to(x[:, :, None, :], (nv, halves, nseq, n_heads)).reshape(nv, LANES)


def _mods(ada_rows, d, per_row, rows_per_seq):
    parts = [ada_rows[:, i * d:(i + 1) * d] for i in range(6)]
    sh_m, sc_m, gt_m, sh_f, sc_f, gt_f = parts
    out = (sh_m, 1.0 + sc_m, 1.0 + gt_m, sh_f, 1.0 + sc_f, 1.0 + gt_f)
    if per_row:
        return tuple(jnp.repeat(z, rows_per_seq, axis=0)[None] for z in out)
    return tuple(z[:, None, :] for z in out)


def _layer(x, h_in, mods, nseq, t_len, lw, attn_fn, states, tiles):
    (sh_m, sc_m, gt_m, sh_f, sc_f, gt_f) = mods
    shift0, wkv0, conv0, ffn0 = states
    d = x.shape[1]
    tm_mm, tm_ew, tm_seq, tpm = tiles["mm"], tiles["ew"], tiles["seq"], tiles["tiles_per_mod"]
    tiles_per_seq = t_len // tm_seq
    rdim = lw["rdim"]
    n_rw_heads = rdim // RWKV_HEAD_DIM

    l = lw["layer"]
    att3, rw_pad = lw["att3"], lw["rw_pad"]
    p_qkv = _matmul([h_in], lw["w_in_t"], l, 0, att3, tm_mm, _pick_tile(att3, 512), w_is_t=True)
    p_rw = _matmul([h_in], lw["w_in_t"], l, att3, rw_pad, tm_mm, _pick_tile(math.gcd(att3, rw_pad), 512),
                   w_is_t=True)
    n_cv = lw["w_cv_t"].shape[1]
    p_cv = _matmul([h_in], lw["w_cv_t"], 0, 0, n_cv, tm_mm, _pick_tile(n_cv, 512), w_is_t=True)

    y_att, k_new, v_new = attn_fn(p_qkv)

    r, k, v, a, dec, g = _rw_prep(p_rw, shift0, lw["mu"], lw["w0"], lw["wup"], lw["a0"], lw["aup"],
                                  lw["gup"], tm_seq, tiles_per_seq, rdim, lw["lora_ab"])
    chains, halves, nv = _chain_geometry(nseq, n_rw_heads)
    ck = lambda z: _to_chain_k(z, nseq, t_len, n_rw_heads, halves)
    per_head = lambda z: z.reshape(n_rw_heads, RWKV_HEAD_DIM)
    o_c, s_c = _rw_scan(ck(r), ck(k), ck(a), ck(dec),
                        _to_chain_v(v, nseq, t_len, n_rw_heads, halves, nv),
                        _head_tile_k(per_head(lw["k_k"]), nseq, halves),
                        _head_tile_k(per_head(lw["k_a"]), nseq, halves),
                        _head_tile_k(lw["r_k"], nseq, halves),
                        _head_tile_v(lw["ln_w"], nseq, n_rw_heads, halves, nv),
                        _head_tile_v(lw["ln_b"], nseq, n_rw_heads, halves, nv),
                        _state_to_chain(wkv0, nseq, n_rw_heads, halves, nv),
                        tiles["scan"], halves, chains, halves == 2)
    o_rw = _from_chain_v(o_c, nseq, t_len, n_rw_heads, halves, nv)
    wkv1 = _state_from_chain(s_c, nseq, n_rw_heads, halves, nv)
    shift1 = p_rw.reshape(nseq, t_len, -1)[:, -1, :lw["rw_proj"]]

    y_cv, y_rw, conv1 = _conv_mix(p_cv, conv0, lw["conv_w"], o_rw, g, tm_seq, tiles_per_seq)

    mix = _matmul([y_att, y_rw, y_cv], lw["w_o"], l, 0, d, tm_mm, _pick_tile(d, 512))
    x1, h2 = _resid(mix, x, lw["g_post_mix"], gt_m, tm_ew, tpm, nxt=(lw["g_pre_ffn"], sc_f, sh_f))

    dff = lw["ffn_conv_w"].shape[1]
    f, ffn1 = _ffn_in(h2, lw["w_ffn_in"], l, ffn0, lw["ffn_conv_w"], tiles["ffn"], _pick_tile(dff, 256), t_len)
    fo = _matmul([f], lw["w_ffn_out"], 0, 0, d, tiles["mm_wide_k"], _pick_tile(d, 256))
    return x1, fo, gt_f, (k_new, v_new, wkv1, shift1, conv1, ffn1)


def kernel(x_prompt, x_sample, c_prompt, c_sample, cache_k, cache_v, state_wkv, state_shift, state_conv, state_ffn, page_table, w_ada, b_ada, g_pre_mix, g_post_mix, g_pre_ffn, g_post_ffn, w_in, rw_mu, rw_w0, rw_w_up, rw_a0, rw_a_up, rw_g_up, rw_k_k, rw_k_a, rw_r_k, rw_ln_w, rw_ln_b, conv_w, w_o, w_ffn_in, ffn_conv_w, w_ffn_out):
    nb_p, s_len, d = x_prompt.shape
    nb_s, t_s, _ = x_sample.shape
    depth = w_in.shape[0]
    att = cache_k.shape[2] * HEAD_DIM
    n_heads = att // HEAD_DIM
    rdim = rw_w0.shape[1]
    rw_proj = rw_mu.shape[1]
    lora_ab = rw_w_up.shape[1] + rw_a_up.shape[1]
    lora_g = rw_g_up.shape[1]
    lora_g_pad = _round_up(lora_g, LANES)
    rw_pad = _round_up(3 * rdim + lora_ab + lora_g_pad, 512)
    lora_g_pad = rw_pad - 3 * rdim - lora_ab
    assert lora_ab == LANES
    past_len = page_table.shape[1] * PAGE_SIZE
    assert past_len % MOBA_BLOCK == 0 and s_len % MOBA_BLOCK == 0
    n_full = past_len // MOBA_BLOCK
    conv_dim = conv_w.shape[2]
    dff = ffn_conv_w.shape[2]

    xp = x_prompt.reshape(nb_p * s_len, d)
    xs = x_sample.reshape(nb_s * t_s, d)
    n_c = nb_p + nb_s
    c_all = jnp.concatenate([c_prompt, c_sample, jnp.zeros((_round_up(n_c, SUBLANES) - n_c, d), F32)], 0)

    cos_p, sin_p = _rope_tables(jnp.arange(s_len, dtype=jnp.int32))
    cos_s, sin_s = _rope_tables(past_len + jnp.arange(t_s, dtype=jnp.int32))

    tiles_p = dict(mm=_pick_tile(nb_p * s_len, 1024), mm_wide_k=_pick_tile(nb_p * s_len, 512),
                   ffn=_pick_tile(s_len, 512), ew=MOBA_BLOCK, seq=MOBA_BLOCK, scan=MOBA_BLOCK // 4,
                   tiles_per_mod=s_len // MOBA_BLOCK)
    tiles_s = dict(mm=nb_s * t_s, mm_wide_k=nb_s * t_s, ffn=nb_s * t_s, ew=nb_s * t_s, seq=t_s, scan=t_s,
                   tiles_per_mod=1)

    zeros_p = (jnp.zeros((nb_p, 1, rw_pad), F32),
               jnp.zeros((nb_p, rdim // RWKV_HEAD_DIM, RWKV_HEAD_DIM, RWKV_HEAD_DIM), F32),
               jnp.zeros((nb_p, 2, conv_dim), F32), jnp.zeros((nb_p, 2, dff), F32))

    w_in_t = jnp.swapaxes(w_in, 1, 2)

    def layer_weights(l):
        pad_rows = lambda w, before, total: jnp.pad(w, ((before, total - before - w.shape[0]), (0, 0)))
        return dict(
            layer=l, att3=3 * att, rw_pad=rw_pad,
            w_in_t=w_in_t, w_cv_t=w_in_t[l, 3 * att + rw_proj:, :][None],
            w_o=w_o, w_ffn_in=w_ffn_in, w_ffn_out=w_ffn_out[l].astype(BF16)[None],
            mu=jnp.pad(rw_mu[l], (0, rw_pad - rw_proj)), w0=rw_w0[l], a0=rw_a0[l],
            wup=pad_rows(rw_w_up[l], 0, lora_ab).astype(BF16),
            aup=pad_rows(rw_a_up[l], rw_w_up.shape[1], lora_ab).astype(BF16),
            gup=pad_rows(rw_g_up[l], 0, lora_g_pad).astype(BF16),
            k_k=rw_k_k[l], k_a=rw_k_a[l], r_k=rw_r_k[l], ln_w=rw_ln_w[l], ln_b=rw_ln_b[l],
            conv_w=conv_w[l], ffn_conv_w=ffn_conv_w[l],
            g_post_mix=g_post_mix[l], g_pre_ffn=g_pre_ffn[l],
            rdim=rdim, rw_proj=rw_proj, lora_ab=lora_ab)

    def attn_prompt(p_qkv):
        return _moba_prompt(p_qkv, nb_p, s_len, n_heads, cos_p, sin_p)

    def make_attn_sample(l):
        def fn(p_qkv):
            q_s, k_s, v_s = _rope_sample(p_qkv, nb_s, t_s, n_heads, cos_s, sin_s)
            means = _block_means(cache_k, l, page_table, n_full)
            idx = _topk_sample(q_s, means, n_full)
            idx_flat = idx[..., :MOBA_TOPK].reshape(-1)
            y = _attn_sample(q_s, k_s, v_s, cache_k, cache_v, l, idx_flat, page_table)
            return y.astype(BF16), k_s, v_s
        return fn

    outs_p, outs_s = [], []
    hp = hs = None
    fo_p = fo_s = gtf_p = gtf_s = None
    g_post_prev = None
    for l in range(depth):
        ada = _ada(c_all, w_ada, l, b_ada[l])
        mods_p = _mods(ada[:nb_p], d, False, s_len)
        mods_s = _mods(ada[nb_p:n_c], d, True, t_s)
        if l == 0:
            hp = _norm_mod(xp, g_pre_mix[l], mods_p[1], mods_p[0], tiles_p["ew"], tiles_p["tiles_per_mod"])
            hs = _norm_mod(xs, g_pre_mix[l], mods_s[1], mods_s[0], tiles_s["ew"], 1)
        else:
            xp, hp = _resid(fo_p, xp, g_post_prev, gtf_p, tiles_p["ew"], tiles_p["tiles_per_mod"],
                            nxt=(g_pre_mix[l], mods_p[1], mods_p[0]))
            xs, hs = _resid(fo_s, xs, g_post_prev, gtf_s, tiles_s["ew"], 1,
                            nxt=(g_pre_mix[l], mods_s[1], mods_s[0]))
        lw = layer_weights(l)
        xp, fo_p, gtf_p, st_p = _layer(xp, hp, mods_p, nb_p, s_len, lw, attn_prompt, zeros_p, tiles_p)
        states_s = (jnp.pad(state_shift[l], ((0, 0), (0, rw_pad - rw_proj)))[:, None, :], state_wkv[l],
                    state_conv[l], state_ffn[l])
        xs, fo_s, gtf_s, st_s = _layer(xs, hs, mods_s, nb_s, t_s, lw, make_attn_sample(l), states_s, tiles_s)
        g_post_prev = g_post_ffn[l]
        outs_p.append(st_p)
        outs_s.append(st_s)
    xp, _ = _resid(fo_p, xp, g_post_prev, gtf_p, tiles_p["ew"], tiles_p["tiles_per_mod"])
    xs, _ = _resid(fo_s, xs, g_post_prev, gtf_s, tiles_s["ew"], 1)

    stack = lambda sts, i: jnp.stack([st[i] for st in sts])
    return (xp.reshape(nb_p, s_len, d), xs.reshape(nb_s, t_s, d),
            stack(outs_p, 0), stack(outs_p, 1), stack(outs_s, 0), stack(outs_s, 1),
            stack(outs_p, 2), stack(outs_s, 2), stack(outs_p, 3), stack(outs_s, 3),
            stack(outs_p, 4), stack(outs_s, 4), stack(outs_p, 5), stack(outs_s, 5))
```

```python
import functools
import math

import jax
import jax.numpy as jnp
from jax import lax
from jax.experimental import pallas as pl
from jax.experimental.pallas import tpu as pltpu

F32 = jnp.float32
BF16 = jnp.bfloat16

LANES = 128
SUBLANES = 8
VMEM_LIMIT_BYTES = 56 * 1024 * 1024

HEAD_DIM = 128
RWKV_HEAD_DIM = 64
PAGE_SIZE = 128
MOBA_BLOCK = 256
MOBA_TOPK = 3
PAGES_PER_BLOCK = MOBA_BLOCK // PAGE_SIZE
ROPE_THETA = 10000.0
RMS_EPS = 1e-6
GN_EPS = 64e-5
NEG = -1e30
ATT_SCALE = HEAD_DIM ** -0.5

_NT = (((1,), (1,)), ((), ()))


def _cparams(*sem):
    return pltpu.CompilerParams(dimension_semantics=sem, vmem_limit_bytes=VMEM_LIMIT_BYTES)


def _round_up(x, m):
    return (x + m - 1) // m * m


def _pick_tile(n, pref):
    if n <= pref:
        return n
    t = pref
    while t >= LANES:
        if n % t == 0:
            return t
        t -= LANES
    return n


def _sigmoid(x):
    return 1.0 / (1.0 + jnp.exp(-x))


def _split_bf16(x):
    hi = x.astype(BF16)
    lo = (x - hi.astype(F32)).astype(BF16)
    return hi, lo


def _dot3_nt(a, b):
    ah, al = _split_bf16(a)
    bh, bl = _split_bf16(b)
    dg = functools.partial(lax.dot_general, dimension_numbers=_NT, preferred_element_type=F32)
    return dg(ah, bh) + dg(ah, bl) + dg(al, bh)


def _top3_lowest_index(gate, cand, lane):
    sel = jnp.zeros(gate.shape, F32)
    lane = lane.astype(F32)
    firsts = []
    for _ in range(MOBA_TOPK):
        gm = jnp.where(cand > 0.5, gate, -jnp.inf)
        mx = jnp.max(gm, axis=1, keepdims=True)
        is_max = jnp.where(cand > 0.5, jnp.where(gm == mx, 1.0, 0.0), 0.0)
        first = jnp.min(jnp.where(is_max > 0.5, lane, float(LANES)), axis=1, keepdims=True)
        pick = jnp.where(lane == first, 1.0, 0.0)
        sel = sel + pick
        cand = cand - pick
        firsts.append(first)
    return sel, firsts


def _ada_body(c_ref, w_ref, b_ref, o_ref):
    c = c_ref[...]
    s = c * _sigmoid(c)
    o_ref[...] = jnp.dot(s.astype(BF16), w_ref[...].astype(BF16),
                         preferred_element_type=F32) + b_ref[...]


def _ada(c_pad, w_all, layer, b):
    _, d, n = w_all.shape
    tn = _pick_tile(n, 512)
    return pl.pallas_call(
        _ada_body,
        grid=(n // tn,),
        in_specs=[pl.BlockSpec(c_pad.shape, lambda j: (0, 0)),
                  pl.BlockSpec((None, d, tn), lambda j: (layer, 0, j)),
                  pl.BlockSpec((1, tn), lambda j: (0, j))],
        out_specs=pl.BlockSpec((c_pad.shape[0], tn), lambda j: (0, j)),
        out_shape=jax.ShapeDtypeStruct((c_pad.shape[0], n), F32),
        compiler_params=_cparams("parallel"),
        name="ada",
    )(c_pad, w_all, b.reshape(1, n))


def _rms(x, g):
    ms = jnp.mean(x * x, axis=-1, keepdims=True)
    return x * lax.rsqrt(ms + RMS_EPS) * g


def _norm_mod_body(x_ref, g_ref, sc_ref, sh_ref, h_ref):
    h_ref[...] = (_rms(x_ref[...], g_ref[...]) * sc_ref[...] + sh_ref[...]).astype(h_ref.dtype)


def _mod_spec(mod, tiles_per_mod):
    return pl.BlockSpec((None,) + mod.shape[1:], lambda i: (i // tiles_per_mod, 0, 0))


def _norm_mod(x, g, sc1p, sh, tm, tiles_per_mod):
    m, d = x.shape
    row = pl.BlockSpec((tm, d), lambda i: (i, 0))
    return pl.pallas_call(
        _norm_mod_body,
        grid=(m // tm,),
        in_specs=[row, pl.BlockSpec((1, d), lambda i: (0, 0)),
                  _mod_spec(sc1p, tiles_per_mod), _mod_spec(sh, tiles_per_mod)],
        out_specs=row,
        out_shape=jax.ShapeDtypeStruct((m, d), BF16),
        compiler_params=_cparams("parallel"),
        name="norm_mod",
    )(x, g.reshape(1, d), sc1p, sh)


def _resid_body(*refs, with_h):
    if with_h:
        y_ref, x_ref, gp_ref, gate_ref, gn_ref, sc_ref, sh_ref, xo_ref, h_ref = refs
    else:
        y_ref, x_ref, gp_ref, gate_ref, xo_ref = refs
    xn = x_ref[...] + gate_ref[...] * _rms(y_ref[...], gp_ref[...])
    xo_ref[...] = xn
    if with_h:
        h_ref[...] = (_rms(xn, gn_ref[...]) * sc_ref[...] + sh_ref[...]).astype(h_ref.dtype)


def _resid(y, x, g_post, gate1p, tm, tiles_per_mod, nxt=None):
    m, d = x.shape
    row = pl.BlockSpec((tm, d), lambda i: (i, 0))
    vec = pl.BlockSpec((1, d), lambda i: (0, 0))
    ins = [y, x, g_post.reshape(1, d), gate1p]
    specs = [row, row, vec, _mod_spec(gate1p, tiles_per_mod)]
    out_shape = [jax.ShapeDtypeStruct((m, d), F32)]
    out_specs = [row]
    if nxt is not None:
        g_next, sc1p, sh = nxt
        ins += [g_next.reshape(1, d), sc1p, sh]
        specs += [vec, _mod_spec(sc1p, tiles_per_mod), _mod_spec(sh, tiles_per_mod)]
        out_shape.append(jax.ShapeDtypeStruct((m, d), BF16))
        out_specs.append(row)
    res = pl.pallas_call(
        functools.partial(_resid_body, with_h=nxt is not None),
        grid=(m // tm,),
        in_specs=specs,
        out_specs=out_specs,
        out_shape=out_shape,
        compiler_params=_cparams("parallel"),
        name="resid",
    )(*ins)
    return res if nxt is not None else (res[0], None)


def _mm_body(*refs, k_sizes, cast_w, w_is_t, with_extra):
    n_lhs = len(k_sizes)
    n_in = n_lhs * (2 if with_extra else 1)
    w_ref, o_ref = refs[n_in], refs[n_in + 1]
    scratch_at = n_in + (3 if with_extra else 2)
    if cast_w:
        wbf = refs[scratch_at]

        @pl.when(pl.program_id(1) == 0)
        def _():
            w = w_ref[...]
            wbf[...] = (w.T if w_is_t else w).astype(BF16)

        w_ref = wbf

    def product(lhs_refs):
        acc = None
        off = 0
        for a_ref, ks in zip(lhs_refs, k_sizes):
            part = jnp.dot(a_ref[...], w_ref[off:off + ks, :], preferred_element_type=F32)
            acc = part if acc is None else acc + part
            off += ks
        return acc

    o_ref[...] = product(refs[:n_lhs]).astype(o_ref.dtype)
    if with_extra:
        @pl.when(pl.program_id(1) == pl.num_programs(1) - 1)
        def _():
            refs[n_in + 2][...] = product(refs[n_lhs:n_in])


def _matmul(lhs_list, w_all, layer, col0, n, tm, tn, w_is_t=False, extra=None):
    m = lhs_list[0].shape[0]
    k_sizes = tuple(a.shape[1] for a in lhs_list)
    k = w_all.shape[2 if w_is_t else 1]
    assert sum(k_sizes) == k and m % tm == 0 and n % tn == 0 and col0 % tn == 0
    assert col0 + n <= w_all.shape[1 if w_is_t else 2]
    cast_w = w_all.dtype != BF16
    assert cast_w or not w_is_t
    j0 = col0 // tn
    w_spec = (pl.BlockSpec((None, tn, k), lambda j, i: (layer, j0 + j, 0)) if w_is_t
              else pl.BlockSpec((None, k, tn), lambda j, i: (layer, 0, j0 + j)))
    in_specs = [pl.BlockSpec((tm, ks), lambda j, i: (i, 0)) for ks in k_sizes]
    out_specs = [pl.BlockSpec((tm, tn), lambda j, i: (i, j))]
    out_shape = [jax.ShapeDtypeStruct((m, n), F32)]
    operands = list(lhs_list)
    if extra is not None:
        me = extra[0].shape[0]
        assert tuple(a.shape[1] for a in extra) == k_sizes
        in_specs += [pl.BlockSpec((me, ks), lambda j, i: (0, 0)) for ks in k_sizes]
        out_specs.append(pl.BlockSpec((me, tn), lambda j, i: (0, j)))
        out_shape.append(jax.ShapeDtypeStruct((me, n), F32))
        operands += list(extra)
    res = pl.pallas_call(
        functools.partial(_mm_body, k_sizes=k_sizes, cast_w=cast_w, w_is_t=w_is_t,
                          with_extra=extra is not None),
        grid=(n // tn, m // tm),
        in_specs=in_specs + [w_spec],
        out_specs=out_specs,
        out_shape=out_shape,
        scratch_shapes=[pltpu.VMEM((k, tn), BF16)] if cast_w else [],
        compiler_params=_cparams("parallel", "arbitrary"),
        name="matmul",
    )(*operands, w_all)
    return (res[0], res[1]) if extra is not None else (res[0], None)


def _shifted_rows(u, halo, state, first):
    p1 = jnp.where(first, state[1:2, :], halo[SUBLANES - 1:SUBLANES, :])
    p2 = jnp.where(first, state[0:1, :], halo[SUBLANES - 2:SUBLANES - 1, :])
    row = lax.broadcasted_iota(jnp.int32, u.shape, 0)
    u1 = jnp.where(row == 0, p1, pltpu.roll(u, 1, 0))
    u2 = jnp.where(row == 0, p2, jnp.where(row == 1, p1, pltpu.roll(u, 2, 0)))
    return u1, u2


def _shifted_rows_short(u, prev1, prev2, t_len):
    pos = lax.broadcasted_iota(jnp.int32, u.shape, 0) % t_len
    u1 = jnp.where(pos == 0, prev1, pltpu.roll(u, 1, 0))
    u2 = jnp.where(pos == 0, prev2, jnp.where(pos == 1, prev1, pltpu.roll(u, 2, 0)))
    return u1, u2


def _ffn_in_body(*refs, tiles_per_seq, t_len):
    short = tiles_per_seq == 0
    if short:
        h_ref, wg_ref, wu_ref, p1_ref, p2_ref, cw_ref, f_ref, u_ref, wgb, wub = refs
    else:
        h_ref, wg_ref, wu_ref, s0_ref, cw_ref, f_ref, s1_ref, wgb, wub, tail = refs
    i = pl.program_id(1)

    @pl.when(i == 0)
    def _():
        wgb[...] = wg_ref[...].astype(BF16)
        wub[...] = wu_ref[...].astype(BF16)
        if not short:
            tail[...] = jnp.zeros(tail.shape, F32)

    h = h_ref[...]
    u = jnp.dot(h, wgb[...], preferred_element_type=F32)
    up = jnp.dot(h, wub[...], preferred_element_type=F32)
    if short:
        u1, u2 = _shifted_rows_short(u, p1_ref[...], p2_ref[...], t_len)
        u_ref[...] = u
    else:
        u1, u2 = _shifted_rows(u, tail[...], s0_ref[...], (i % tiles_per_seq) == 0)
        tail[...] = u[u.shape[0] - SUBLANES:, :]

        @pl.when((i % tiles_per_seq) == tiles_per_seq - 1)
        def _():
            s1_ref[...] = u[u.shape[0] - 2:, :]

    w = cw_ref[...]
    y = u2 * w[0:1, :] + u1 * w[1:2, :] + u * w[2:3, :]
    f_ref[...] = (y * _sigmoid(y) * up).astype(f_ref.dtype)


def _ffn_in(h, w_all, layer, ffn0, ffn_w, tm, tc, t_len):
    m, k = h.shape
    dff = ffn_w.shape[1]
    nseq = ffn0.shape[0]
    ncol = dff // tc
    assert dff % tc == 0 and m % tm == 0
    short = tm > t_len
    tiles_per_seq = 0 if short else t_len // tm
    lhs = pl.BlockSpec((tm, k), lambda j, i: (i, 0))
    wspec = lambda off: pl.BlockSpec((None, k, tc), lambda j, i: (layer, 0, off + j))
    tile = pl.BlockSpec((tm, tc), lambda j, i: (i, j))
    cw = pl.BlockSpec((3, tc), lambda j, i: (0, j))
    scratch = [pltpu.VMEM((k, tc), BF16), pltpu.VMEM((k, tc), BF16)]
    if short:
        assert tm == m and m == nseq * t_len
        prev1 = jnp.repeat(ffn0[:, 1, :], t_len, axis=0)
        prev2 = jnp.repeat(ffn0[:, 0, :], t_len, axis=0)
        f, u = pl.pallas_call(
            functools.partial(_ffn_in_body, tiles_per_seq=0, t_len=t_len),
            grid=(ncol, 1),
            in_specs=[lhs, wspec(0), wspec(ncol), tile, tile, cw],
            out_specs=[tile, tile],
            out_shape=[jax.ShapeDtypeStruct((m, dff), BF16), jax.ShapeDtypeStruct((m, dff), F32)],
            scratch_shapes=scratch,
            compiler_params=_cparams("parallel", "arbitrary"),
            name="ffn_in",
        )(h, w_all, w_all, prev1, prev2, ffn_w)
        return f, u.reshape(nseq, t_len, dff)[:, t_len - 2:, :]
    state = pl.BlockSpec((None, 2, tc), lambda j, i: (i // tiles_per_seq, 0, j))
    return pl.pallas_call(
        functools.partial(_ffn_in_body, tiles_per_seq=tiles_per_seq, t_len=t_len),
        grid=(ncol, m // tm),
        in_specs=[lhs, wspec(0), wspec(ncol), state, cw],
        out_specs=[tile, state],
        out_shape=[jax.ShapeDtypeStruct((m, dff), BF16), jax.ShapeDtypeStruct((nseq, 2, dff), F32)],
        scratch_shapes=scratch + [pltpu.VMEM((SUBLANES, tc), F32)],
        compiler_params=_cparams("parallel", "arbitrary"),
        name="ffn_in",
    )(h, w_all, w_all, ffn0, ffn_w)


def _rope_tables(pos):
    half = HEAD_DIM // 2
    inv = ROPE_THETA ** (-jnp.arange(half, dtype=F32) / half)
    ang = pos.astype(F32)[:, None] * inv[None, :]
    cos, sin = jnp.cos(ang), jnp.sin(ang)
    return jnp.concatenate([cos, cos], axis=1), jnp.concatenate([-sin, sin], axis=1)


def _rope(x, cos2, sin2):
    return x * cos2 + pltpu.roll(x, HEAD_DIM // 2, 1) * sin2


def _moba_p_body(q_ref, k_ref, v_ref, cos_ref, sin_ref, e_ref, y_ref, ko_ref, vo_ref, kext, vbf, means):
    s_len = k_ref.shape[0]
    nb = s_len // MOBA_BLOCK
    n_pages = s_len // PAGE_SIZE

    kr = _rope(k_ref[...], cos_ref[...], sin_ref[...])
    v = v_ref[...]
    ko_ref[...] = kr.reshape(n_pages, PAGE_SIZE, HEAD_DIM)
    vo_ref[...] = v.reshape(n_pages, PAGE_SIZE, HEAD_DIM)
    kext[:, 0:HEAD_DIM] = kr.astype(BF16)
    kext[:, HEAD_DIM:] = e_ref[...]
    vbf[...] = v.astype(BF16)
    means[...] = jnp.zeros(means.shape, F32)
    means[0:nb, :] = jnp.mean(kr.reshape(nb, MOBA_BLOCK, HEAD_DIM), axis=1)

    lane = lax.broadcasted_iota(jnp.int32, (MOBA_BLOCK, LANES), 1)
    row = lax.broadcasted_iota(jnp.int32, (MOBA_BLOCK, MOBA_BLOCK), 0)
    col = lax.broadcasted_iota(jnp.int32, (MOBA_BLOCK, MOBA_BLOCK), 1)
    for n in range(nb):
        rows = slice(n * MOBA_BLOCK, (n + 1) * MOBA_BLOCK)
        qr = _rope(q_ref[rows, :], cos_ref[rows, :], sin_ref[rows, :])
        keep = jnp.where(lane == n, 1.0, 0.0)
        if n > 0:
            gate = _dot3_nt(qr, means[...])
            cand = jnp.where(lane < n, 1.0, 0.0)
            sel, _ = _top3_lowest_index(gate, cand, lane)
            keep = keep + sel
        qext = jnp.concatenate([qr.astype(BF16), (keep - 1.0).astype(BF16)], axis=1)
        end = (n + 1) * MOBA_BLOCK
        s = lax.dot_general(qext, kext[0:end, :], _NT, preferred_element_type=F32) * ATT_SCALE
        s_own = jnp.where(col <= row, s[:, n * MOBA_BLOCK:end], NEG)
        mx = jnp.max(s_own, axis=1, keepdims=True)
        if n > 0:
            s_past = s[:, 0:n * MOBA_BLOCK]
            mx = jnp.maximum(mx, jnp.max(s_past, axis=1, keepdims=True))
        p_own = jnp.exp(s_own - mx)
        l = jnp.sum(p_own, axis=1, keepdims=True)
        o = jnp.dot(p_own.astype(BF16), vbf[n * MOBA_BLOCK:end, :], preferred_element_type=F32)
        if n > 0:
            p_past = jnp.exp(s_past - mx)
            l = l + jnp.sum(p_past, axis=1, keepdims=True)
            o = o + jnp.dot(p_past.astype(BF16), vbf[0:n * MOBA_BLOCK, :], preferred_element_type=F32)
        y_ref[rows, :] = (o / l).astype(y_ref.dtype)


MASK_BIG = 1e30


def _moba_prompt(p_qkv, nseq, s_len, n_heads, cos2, sin2):
    n_pages = s_len // PAGE_SIZE
    m = nseq * s_len
    e = jnp.where(jnp.arange(s_len)[:, None] // MOBA_BLOCK == jnp.arange(LANES)[None, :],
                  MASK_BIG, 0.0).astype(BF16)
    full = lambda off: pl.BlockSpec((s_len, HEAD_DIM), lambda b, h: (b, off + h))
    tab = pl.BlockSpec((s_len, HEAD_DIM), lambda b, h: (0, 0))
    pages = pl.BlockSpec((None, n_pages, None, PAGE_SIZE, HEAD_DIM), lambda b, h: (b, 0, h, 0, 0))
    page_shape = jax.ShapeDtypeStruct((nseq, n_pages, n_heads, PAGE_SIZE, HEAD_DIM), F32)
    return pl.pallas_call(
        _moba_p_body,
        grid=(nseq, n_heads),
        in_specs=[full(0), full(n_heads), full(2 * n_heads), tab, tab, tab],
        out_specs=[full(0), pages, pages],
        out_shape=[jax.ShapeDtypeStruct((m, n_heads * HEAD_DIM), BF16), page_shape, page_shape],
        scratch_shapes=[pltpu.VMEM((s_len, HEAD_DIM + LANES), BF16), pltpu.VMEM((s_len, HEAD_DIM), BF16),
                        pltpu.VMEM((LANES, HEAD_DIM), F32)],
        compiler_params=_cparams("parallel", "parallel"),
        name="moba_prompt",
    )(p_qkv, p_qkv, p_qkv, cos2, sin2, e)


def _rope_s_body(p_ref, cos_ref, sin_ref, q_o, k_o, v_o, *, n_heads):
    cos2, sin2 = cos_ref[...], sin_ref[...]
    for h in range(n_heads):
        sl = lambda part: p_ref[:, (part * n_heads + h) * HEAD_DIM:(part * n_heads + h + 1) * HEAD_DIM]
        q_o[h] = _rope(sl(0), cos2, sin2)
        k_o[h] = _rope(sl(1), cos2, sin2)
        v_o[h] = sl(2)


def _rope_sample(p_qkv, nseq, t_len, n_heads, cos2, sin2):
    shp = jax.ShapeDtypeStruct((nseq, n_heads, t_len, HEAD_DIM), F32)
    out = pl.BlockSpec((None, n_heads, t_len, HEAD_DIM), lambda b: (b, 0, 0, 0))
    tab = pl.BlockSpec((t_len, HEAD_DIM), lambda b: (0, 0))
    return pl.pallas_call(
        functools.partial(_rope_s_body, n_heads=n_heads),
        grid=(nseq,),
        in_specs=[pl.BlockSpec((t_len, p_qkv.shape[1]), lambda b: (b, 0)), tab, tab],
        out_specs=[out, out, out],
        out_shape=[shp, shp, shp],
        compiler_params=_cparams("parallel"),
        name="rope_sample",
    )(p_qkv, cos2, sin2)


_MEANS_BLOCKS_PER_STEP = 4


def _means_body(pt_ref, *refs, n_heads):
    page_refs, o_ref = refs[:-1], refs[-1]
    j = pl.program_id(1)

    @pl.when(j == 0)
    def _():
        o_ref[...] = jnp.zeros(o_ref.shape, F32)

    for blk in range(_MEANS_BLOCKS_PER_STEP):
        s = jnp.sum(page_refs[PAGES_PER_BLOCK * blk][...], axis=1)
        for pg in range(1, PAGES_PER_BLOCK):
            s = s + jnp.sum(page_refs[PAGES_PER_BLOCK * blk + pg][...], axis=1)
        s = s * (1.0 / MOBA_BLOCK)
        for h in range(n_heads):
            o_ref[h, pl.ds(j * _MEANS_BLOCKS_PER_STEP + blk, 1), :] = s[h:h + 1, :]


def _block_means(cache_k, layer, page_table, n_full):
    nseq = page_table.shape[0]
    n_heads = cache_k.shape[2]
    pages_per_step = PAGES_PER_BLOCK * _MEANS_BLOCKS_PER_STEP
    assert n_full <= LANES and n_full % _MEANS_BLOCKS_PER_STEP == 0
    page = lambda pg: pl.BlockSpec(
        (None, None, n_heads, PAGE_SIZE, HEAD_DIM),
        lambda b, j, pt: (layer, pt[b, pages_per_step * j + pg], 0, 0, 0))
    return pl.pallas_call(
        functools.partial(_means_body, n_heads=n_heads),
        grid_spec=pltpu.PrefetchScalarGridSpec(
            num_scalar_prefetch=1,
            grid=(nseq, n_full // _MEANS_BLOCKS_PER_STEP),
            in_specs=[page(pg) for pg in range(pages_per_step)],
            out_specs=pl.BlockSpec((None, n_heads, LANES, HEAD_DIM), lambda b, j, pt: (b, 0, 0, 0)),
        ),
        out_shape=jax.ShapeDtypeStruct((nseq, n_heads, LANES, HEAD_DIM), F32),
        compiler_params=_cparams("parallel", "arbitrary"),
        name="block_means",
    )(page_table, *([cache_k] * pages_per_step))


def _topk_s_body(q_ref, m_ref, o_ref, *, n_heads, n_full):
    for h in range(n_heads):
        gate = _dot3_nt(q_ref[h], m_ref[h])
        lane = lax.broadcasted_iota(jnp.int32, gate.shape, 1)
        cand = jnp.where(lane < n_full, 1.0, 0.0)
        _, firsts = _top3_lowest_index(gate, cand, lane)
        idx = jnp.zeros(gate.shape, F32)
        for r, first in enumerate(firsts):
            idx = jnp.where(lane == r, first, idx)
        o_ref[h] = idx.astype(jnp.int32)


def _topk_sample(q_s, means, n_full):
    nseq, n_heads, t_len, _ = q_s.shape
    return pl.pallas_call(
        functools.partial(_topk_s_body, n_heads=n_heads, n_full=n_full),
        grid=(nseq,),
        in_specs=[pl.BlockSpec((None, n_heads, t_len, HEAD_DIM), lambda b: (b, 0, 0, 0)),
                  pl.BlockSpec((None, n_heads, LANES, HEAD_DIM), lambda b: (b, 0, 0, 0))],
        out_specs=pl.BlockSpec((None, n_heads, t_len, LANES), lambda b: (b, 0, 0, 0)),
        out_shape=jax.ShapeDtypeStruct((nseq, n_heads, t_len, LANES), jnp.int32),
        compiler_params=_cparams("parallel"),
        name="topk_sample",
    )(q_s, means)


_N_SEL_PAGES = MOBA_TOPK * PAGES_PER_BLOCK


def _attn_s_body(idx_ref, pt_ref, q_ref, kn_ref, vn_ref, *refs):
    q = q_ref[...]
    t_len = q.shape[0]
    n_pg = t_len * _N_SEL_PAGES
    k_refs, v_refs, o_ref = refs[:n_pg], refs[n_pg:2 * n_pg], refs[-1]
    kn, vn = kn_ref[...], vn_ref[...]
    key_new = lax.broadcasted_iota(jnp.int32, (t_len, 1), 0)
    outs = []
    for t in range(t_len):
        q_t = q[t:t + 1, :]
        pages = slice(t * _N_SEL_PAGES, (t + 1) * _N_SEL_PAGES)
        score = lambda keys: jnp.sum(keys * q_t, axis=1, keepdims=True) * ATT_SCALE
        s_all = [score(k[...]) for k in k_refs[pages]]
        s_all.append(jnp.where(key_new <= t, score(kn), NEG))
        mx = jnp.max(s_all[0], axis=0, keepdims=True)
        for s in s_all[1:]:
            mx = jnp.maximum(mx, jnp.max(s, axis=0, keepdims=True))
        l = jnp.zeros((1, 1), F32)
        o = jnp.zeros((1, HEAD_DIM), F32)
        for s, v in zip(s_all, [v[...] for v in v_refs[pages]] + [vn]):
            p = jnp.exp(s - mx)
            l = l + jnp.sum(p, axis=0, keepdims=True)
            o = o + jnp.sum(p * v, axis=0, keepdims=True)
        outs.append(o / l)
    o_ref[...] = jnp.concatenate(outs, axis=0)


def _attn_sample(q_s, k_new, v_new, cache_k, cache_v, layer, idx_flat, page_table):
    nseq, n_heads, t_len, _ = q_s.shape

    def page(t, s, pg):
        def imap(b, h, idx, pt):
            blk = idx[((b * n_heads + h) * t_len + t) * MOBA_TOPK + s]
            return (layer, pt[b, blk * PAGES_PER_BLOCK + pg], h, 0, 0)
        return pl.BlockSpec((None, None, None, PAGE_SIZE, HEAD_DIM), imap)

    pages = [page(t, s, pg) for t in range(t_len) for s in range(MOBA_TOPK) for pg in range(PAGES_PER_BLOCK)]
    new = pl.BlockSpec((None, None, t_len, HEAD_DIM), lambda b, h, idx, pt: (b, h, 0, 0))
    return pl.pallas_call(
        _attn_s_body,
        grid_spec=pltpu.PrefetchScalarGridSpec(
            num_scalar_prefetch=2,
            grid=(nseq, n_heads),
            in_specs=[new, new, new] + pages + pages,
            out_specs=pl.BlockSpec((t_len, HEAD_DIM), lambda b, h, idx, pt: (b, h)),
        ),
        out_shape=jax.ShapeDtypeStruct((nseq * t_len, n_heads * HEAD_DIM), F32),
        compiler_params=_cparams("parallel", "parallel"),
        name="attn_sample",
    )(idx_flat, page_table, q_s, k_new, v_new, *([cache_k] * len(pages)), *([cache_v] * len(pages)))


def _rw_prep_body(p_ref, halo_ref, sp_ref, mu_ref, w0_ref, wup_ref, a0_ref, aup_ref, gup_ref,
                  r_o, k_o, v_o, a_o, d_o, g_o, *, tiles_per_seq, rdim, lora_ab):
    i = pl.program_id(0)
    pf = p_ref[...]
    first = (i % tiles_per_seq) == 0
    prev0 = jnp.where(first, sp_ref[...], halo_ref[SUBLANES - 1:SUBLANES, :])
    row = lax.broadcasted_iota(jnp.int32, pf.shape, 0)
    prev = jnp.where(row == 0, prev0, pltpu.roll(pf, 1, 0))
    pm = pf + (prev - pf) * mu_ref[...]
    r, k, v = pm[:, 0:rdim], pm[:, rdim:2 * rdim], pm[:, 2 * rdim:3 * rdim]
    ca = pm[:, 3 * rdim:3 * rdim + lora_ab]
    cb = pm[:, 3 * rdim + lora_ab:]
    w_lin = jnp.dot(jnp.tanh(ca).astype(BF16), wup_ref[...], preferred_element_type=F32)
    a_lin = jnp.dot(ca.astype(BF16), aup_ref[...], preferred_element_type=F32)
    g = jnp.dot(_sigmoid(cb).astype(BF16), gup_ref[...], preferred_element_type=F32)
    y = -(w0_ref[...] + w_lin)
    w = -(jnp.maximum(y, 0.0) + jnp.log(1.0 + jnp.exp(-jnp.abs(y)))) - 0.5
    a = _sigmoid(a0_ref[...] + a_lin)
    r_o[...] = r
    k_o[...] = k
    v_o[...] = v
    a_o[...] = a
    d_o[...] = jnp.exp(-jnp.exp(w))
    g_o[...] = g


def _rw_prep(p_rw, shift_prev, mu, w0, wup, a0, aup, gup, tm, tiles_per_seq, rdim, lora_ab):
    m, wp = p_rw.shape
    row = pl.BlockSpec((tm, wp), lambda i: (i, 0))
    halo = pl.BlockSpec((SUBLANES, wp), lambda i: (jnp.maximum(i * (tm // SUBLANES) - 1, 0), 0))
    const = lambda a: pl.BlockSpec(a.shape, lambda i: (0,) * a.ndim)
    out = pl.BlockSpec((tm, rdim), lambda i: (i, 0))
    shp = jax.ShapeDtypeStruct((m, rdim), F32)
    vecs = [x.reshape(1, -1) for x in (mu, w0)]
    return pl.pallas_call(
        functools.partial(_rw_prep_body, tiles_per_seq=tiles_per_seq, rdim=rdim, lora_ab=lora_ab),
        grid=(m // tm,),
        in_specs=[row, halo, pl.BlockSpec((None, 1, wp), lambda i: (i // tiles_per_seq, 0, 0)),
                  const(vecs[0]), const(vecs[1]), const(wup), const(a0.reshape(1, -1)), const(aup),
                  const(gup)],
        out_specs=[out] * 6,
        out_shape=[shp] * 6,
        compiler_params=_cparams("parallel"),
        name="rwkv_prep",
    )(p_rw, p_rw, shift_prev, vecs[0], vecs[1], wup, a0.reshape(1, -1), aup, gup)


_SCAN_ROWS_IN_FLIGHT = 4


def _rw_scan_body(r_ref, k_ref, a_ref, d_ref, v_ref, kkt_ref, kat_ref, rk_ref, lnw_ref, lnb_ref, s0_ref,
                  o_ref, so_ref, state, b_s, c_s, kp_s, d_s, r_s, bd_s, al_s, be_s, rk_s, o_s,
                  *, halves, chains, packed):
    pid = pl.program_id(0)
    tc, nv = v_ref.shape[0], v_ref.shape[1]
    n_val = nv * halves
    rows = min(_SCAN_ROWS_IN_FLIGHT, nv)

    @pl.when(pid == 0)
    def _():
        state[...] = s0_ref[...]

    np_ = tc // 2

    def even_odd(ref):
        x = ref[...]
        if packed:
            xr = pltpu.roll(x.reshape(np_ * x.shape[1], LANES), LANES // 2, 1).reshape(x.shape)
            low = lax.broadcasted_iota(jnp.int32, x.shape, 2) < LANES // 2
            return jnp.where(low, x, xr), jnp.where(low, xr, x)
        x = x.reshape((np_, 2) + x.shape[1:])
        return x[:, 0], x[:, 1]

    k2, a2, d2, r2 = even_odd(k_ref), even_odd(a_ref), even_odd(d_ref), even_odd(r_ref)
    b2, c2, kp2 = [], [], []
    for par in range(2):
        k, a = k2[par], a2[par]
        kk = k * kkt_ref[...]
        kkn = kk / jnp.maximum(jnp.sqrt(jnp.sum(kk * kk, axis=1, keepdims=True)), 1e-12)
        kp = k * (1.0 + (a - 1.0) * kat_ref[...])
        b2.append(-kkn)
        c2.append(kkn * a)
        kp2.append(kp)
        b_s[par] = b2[par]
        c_s[par] = c2[par]
        kp_s[par] = kp
        d_s[par] = d2[par]
        r_s[par] = r2[par]
        rk_s[par] = jnp.sum(r2[par] * kp * rk_ref[...], axis=1, keepdims=True)
    bd_s[...] = d2[0] * b2[1]
    al_s[...] = jnp.sum(c2[0] * b2[1], axis=1, keepdims=True)
    be_s[...] = jnp.sum(kp2[0] * b2[1], axis=1, keepdims=True)

    for g in range(nv // rows):
        def step(s, s_rows, g=g):
            t0, t1 = 2 * s, 2 * s + 1
            b0, bd, al, be = b_s[0, s], bd_s[s], al_s[s], be_s[s]
            c0, kp0, d0, r0 = c_s[0, s], kp_s[0, s], d_s[0, s], r_s[0, s]
            c1, kp1, d1, r1 = c_s[1, s], kp_s[1, s], d_s[1, s], r_s[1, s]
            new = []
            for i, sv in enumerate(s_rows):
                vi = g * rows + i
                v0, v1 = v_ref[t0, vi:vi + 1, :], v_ref[t1, vi:vi + 1, :]
                u0 = jnp.sum(sv * b0, axis=0, keepdims=True)
                u1 = jnp.sum(sv * bd, axis=0, keepdims=True) + u0 * al + v0 * be
                s1 = sv * d0 + u0 * c0 + v0 * kp0
                o_s[t0, vi:vi + 1, :] = jnp.sum(s1 * r0, axis=0, keepdims=True)
                s2 = s1 * d1 + u1 * c1 + v1 * kp1
                o_s[t1, vi:vi + 1, :] = jnp.sum(s2 * r1, axis=0, keepdims=True)
                new.append(s2)
            return tuple(new)

        s_rows = lax.fori_loop(0, tc // 2, step, tuple(state[g * rows + i] for i in range(rows)))
        for i, sv in enumerate(s_rows):
            state[g * rows + i] = sv

    def head_sum(x):
        x = x.reshape(tc * nv, LANES)
        shift = chains
        for _ in range(int(math.log2(halves))):
            x = x + pltpu.roll(x, shift, 1)
            shift *= 2
        return jnp.sum(x.reshape(tc, nv, LANES), axis=1, keepdims=True)

    o = o_s[...]
    dev = o - head_sum(o) * (1.0 / n_val)
    var = head_sum(dev * dev) * (1.0 / n_val)
    on = dev * lax.rsqrt(var + GN_EPS) * lnw_ref[...] + lnb_ref[...]
    rk = jnp.stack([rk_s[0], rk_s[1]], axis=1).reshape(tc, 1, LANES)
    o_ref[...] = on + rk * v_ref[...]

    @pl.when(pid == pl.num_programs(0) - 1)
    def _():
        so_ref[...] = state[...]


def _rw_scan(r, k, a, d, v, kk_tile, ka_tile, rk_tile, lnw_tile, lnb_tile, s0, tc, halves, chains, packed):
    t_len, nv = v.shape[0], v.shape[1]
    kdim = r.shape[1]
    assert nv % min(_SCAN_ROWS_IN_FLIGHT, nv) == 0 and t_len % tc == 0 and tc % 2 == 0
    np_ = tc // 2
    ktile = pl.BlockSpec((np_ if packed else tc, kdim, LANES), lambda i: (i, 0, 0))
    vtile = pl.BlockSpec((tc, nv, LANES), lambda i: (i, 0, 0))
    const = lambda x: pl.BlockSpec(x.shape, lambda i: (0,) * x.ndim)
    consts = [kk_tile, ka_tile, rk_tile, lnw_tile, lnb_tile, s0]
    return pl.pallas_call(
        functools.partial(_rw_scan_body, halves=halves, chains=chains, packed=packed),
        grid=(t_len // tc,),
        in_specs=[ktile] * 4 + [vtile] + [const(x) for x in consts],
        out_specs=[vtile, const(s0)],
        out_shape=[jax.ShapeDtypeStruct(v.shape, F32), jax.ShapeDtypeStruct(s0.shape, F32)],
        scratch_shapes=[pltpu.VMEM(s0.shape, F32)] + [pltpu.VMEM((2, np_, kdim, LANES), F32)] * 5
        + [pltpu.VMEM((np_, kdim, LANES), F32)] + [pltpu.VMEM((np_, 1, LANES), F32)] * 2
        + [pltpu.VMEM((2, np_, 1, LANES), F32), pltpu.VMEM((tc, nv, LANES), F32)],
        compiler_params=_cparams("arbitrary"),
        name="rwkv_scan",
    )(r, k, a, d, v, *consts)


def _conv_mix_body(gb_ref, gc_ref, hc_ref, gch_ref, hch_ref, c0_ref, w_ref, o_ref, g_ref,
                   ycv_ref, yrw_ref, c1_ref, *, tiles_per_seq):
    i = pl.program_id(0)
    first = (i % tiles_per_seq) == 0
    u = gc_ref[...] * hc_ref[...]
    u1, u2 = _shifted_rows(u, gch_ref[...] * hch_ref[...], c0_ref[...], first)
    w = w_ref[...]
    y = u2 * w[0:1, :] + u1 * w[1:2, :] + u * w[2:3, :]
    ycv_ref[...] = (gb_ref[...] * y).astype(ycv_ref.dtype)
    yrw_ref[...] = (o_ref[...] * g_ref[...]).astype(yrw_ref.dtype)

    @pl.when((i % tiles_per_seq) == tiles_per_seq - 1)
    def _():
        c1_ref[...] = u[u.shape[0] - 2:, :]


def _conv_mix(p_cv, conv0, conv_w, o_rw, g_rw, tm, tiles_per_seq):
    m = p_cv.shape[0]
    c = conv_w.shape[1]
    nseq = conv0.shape[0]
    col = lambda j: pl.BlockSpec((tm, c), lambda i: (i, j))
    halo = lambda j: pl.BlockSpec((SUBLANES, c), lambda i: (jnp.maximum(i * (tm // SUBLANES) - 1, 0), j))
    state = pl.BlockSpec((None, 2, c), lambda i: (i // tiles_per_seq, 0, 0))
    rw = pl.BlockSpec((tm, o_rw.shape[1]), lambda i: (i, 0))
    return pl.pallas_call(
        functools.partial(_conv_mix_body, tiles_per_seq=tiles_per_seq),
        grid=(m // tm,),
        in_specs=[col(0), col(1), col(2), halo(1), halo(2), state,
                  pl.BlockSpec((3, c), lambda i: (0, 0)), rw, rw],
        out_specs=[col(0), rw, state],
        out_shape=[jax.ShapeDtypeStruct((m, c), BF16), jax.ShapeDtypeStruct(o_rw.shape, BF16),
                   jax.ShapeDtypeStruct((nseq, 2, c), F32)],
        compiler_params=_cparams("arbitrary"),
        name="conv_mix",
    )(p_cv, p_cv, p_cv, p_cv, p_cv, conv0, conv_w, o_rw, g_rw)


def _chain_geometry(nseq, n_heads):
    chains = nseq * n_heads
    assert LANES % chains == 0
    halves = LANES // chains
    assert RWKV_HEAD_DIM % halves == 0
    return chains, halves, RWKV_HEAD_DIM // halves


def _to_chain_k(x, nseq, t_len, n_heads, halves):
    if halves == 2:
        x = x.reshape(nseq, t_len // 2, 2, n_heads, RWKV_HEAD_DIM).transpose(1, 4, 2, 0, 3)
        return x.reshape(t_len // 2, RWKV_HEAD_DIM, LANES)
    x = x.reshape(nseq, t_len, n_heads, RWKV_HEAD_DIM).transpose(1, 3, 0, 2)
    return jnp.tile(x.reshape(t_len, RWKV_HEAD_DIM, nseq * n_heads), (1, 1, halves))


def _to_chain_v(x, nseq, t_len, n_heads, halves, nv):
    x = x.reshape(nseq, t_len, n_heads, halves, nv).transpose(1, 4, 3, 0, 2)
    return x.reshape(t_len, nv, LANES)


def _from_chain_v(x, nseq, t_len, n_heads, halves, nv):
    x = x.reshape(t_len, nv, halves, nseq, n_heads).transpose(3, 0, 4, 2, 1)
    return x.reshape(nseq * t_len, n_heads * RWKV_HEAD_DIM)


def _state_to_chain(s, nseq, n_heads, halves, nv):
    s = s.reshape(nseq, n_heads, halves, nv, RWKV_HEAD_DIM).transpose(3, 4, 2, 0, 1)
    return s.reshape(nv, RWKV_HEAD_DIM, LANES)


def _state_from_chain(s, nseq, n_heads, halves, nv):
    s = s.reshape(nv, RWKV_HEAD_DIM, halves, nseq, n_heads).transpose(3, 4, 2, 0, 1)
    return s.reshape(nseq, n_heads, RWKV_HEAD_DIM, RWKV_HEAD_DIM)


def _head_tile_k(x, nseq, halves):
    return jnp.tile(x.T, (1, halves * nseq))


def _head_tile_v(x, nseq, n_heads, halves, nv):
    x = x.reshape(n_heads, halves, nv).transpose(2, 1, 0)
    return jnp.broadcast_to(x[:, :, None, :], (nv, halves, nseq, n_heads)).reshape(nv, LANES)


def _mods(ada_rows, d, per_row, rows_per_seq):
    parts = [ada_rows[:, i * d:(i + 1) * d] for i in range(6)]
    sh_m, sc_m, gt_m, sh_f, sc_f, gt_f = parts
    out = (sh_m, 1.0 + sc_m, 1.0 + gt_m, sh_f, 1.0 + sc_f, 1.0 + gt_f)
    if per_row:
        return tuple(jnp.repeat(z, rows_per_seq, axis=0)[None] for z in out)
    return tuple(z[:, None, :] for z in out)


def _mixers(p_qkv, p_rw, p_cv, nseq, t_len, lw, attn_fn, states, tiles):
    shift0, wkv0, conv0, _ = states
    tm_seq = tiles["seq"]
    tiles_per_seq = t_len // tm_seq
    rdim = lw["rdim"]
    n_rw_heads = rdim // RWKV_HEAD_DIM

    y_att, k_new, v_new = attn_fn(p_qkv)

    r, k, v, a, dec, g = _rw_prep(p_rw, shift0, lw["mu"], lw["w0"], lw["wup"], lw["a0"], lw["aup"],
                                  lw["gup"], tm_seq, tiles_per_seq, rdim, lw["lora_ab"])
    chains, halves, nv = _chain_geometry(nseq, n_rw_heads)
    ck = lambda z: _to_chain_k(z, nseq, t_len, n_rw_heads, halves)
    per_head = lambda z: z.reshape(n_rw_heads, RWKV_HEAD_DIM)
    o_c, s_c = _rw_scan(ck(r), ck(k), ck(a), ck(dec),
                        _to_chain_v(v, nseq, t_len, n_rw_heads, halves, nv),
                        _head_tile_k(per_head(lw["k_k"]), nseq, halves),
                        _head_tile_k(per_head(lw["k_a"]), nseq, halves),
                        _head_tile_k(lw["r_k"], nseq, halves),
                        _head_tile_v(lw["ln_w"], nseq, n_rw_heads, halves, nv),
                        _head_tile_v(lw["ln_b"], nseq, n_rw_heads, halves, nv),
                        _state_to_chain(wkv0, nseq, n_rw_heads, halves, nv),
                        tiles["scan"], halves, chains, halves == 2)
    o_rw = _from_chain_v(o_c, nseq, t_len, n_rw_heads, halves, nv)
    wkv1 = _state_from_chain(s_c, nseq, n_rw_heads, halves, nv)
    shift1 = p_rw.reshape(nseq, t_len, -1)[:, -1, :lw["rw_proj"]]

    y_cv, y_rw, conv1 = _conv_mix(p_cv, conv0, lw["conv_w"], o_rw, g, tm_seq, tiles_per_seq)
    return [y_att, y_rw, y_cv], (k_new, v_new, wkv1, shift1, conv1)


def _layer(main, extra, lw):
    d = main["x"].shape[1]
    l = lw["layer"]
    tm = main["tiles"]["mm"]
    att3, rw_pad = lw["att3"], lw["rw_pad"]
    both = (main, extra)

    mm_in = lambda w, layer, col0, n, tn: _matmul([main["h"]], w, layer, col0, n, tm, tn, w_is_t=True,
                                                  extra=[extra["h"]])
    p_qkv = mm_in(lw["w_in_t"], l, 0, att3, _pick_tile(att3, 512))
    p_rw = mm_in(lw["w_in_t"], l, att3, rw_pad, _pick_tile(math.gcd(att3, rw_pad), 512))
    n_cv = lw["w_cv_t"].shape[1]
    p_cv = mm_in(lw["w_cv_t"], 0, 0, n_cv, _pick_tile(n_cv, 512))

    ys, sts = [], []
    for gi, g in enumerate(both):
        y, st = _mixers(p_qkv[gi], p_rw[gi], p_cv[gi], g["nseq"], g["t_len"], lw, g["attn"], g["states"],
                        g["tiles"])
        ys.append(y)
        sts.append(st)

    mix = _matmul(ys[0], lw["w_o"], l, 0, d, tm, _pick_tile(d, 512), extra=ys[1])
    dff = lw["ffn_conv_w"].shape[1]
    x1s, fs = [], []
    for gi, g in enumerate(both):
        (_, _, gt_m, sh_f, sc_f, _) = g["mods"]
        t = g["tiles"]
        x1, h2 = _resid(mix[gi], g["x"], lw["g_post_mix"], gt_m, t["ew"], t["tiles_per_mod"],
                        nxt=(lw["g_pre_ffn"], sc_f, sh_f))
        f, ffn1 = _ffn_in(h2, lw["w_ffn_in"], l, g["states"][3], lw["ffn_conv_w"], t["ffn"],
                          _pick_tile(dff, 256), g["t_len"])
        x1s.append(x1)
        fs.append(f)
        sts[gi] = sts[gi] + (ffn1,)
    fo = _matmul([fs[0]], lw["w_ffn_out"], l, 0, d, main["tiles"]["mm_wide_k"], _pick_tile(d, 512),
                 extra=[fs[1]])
    return [(x1s[gi], fo[gi], both[gi]["mods"][5], sts[gi]) for gi in range(2)]


def kernel(x_prompt, x_sample, c_prompt, c_sample, cache_k, cache_v, state_wkv, state_shift, state_conv, state_ffn, page_table, w_ada, b_ada, g_pre_mix, g_post_mix, g_pre_ffn, g_post_ffn, w_in, rw_mu, rw_w0, rw_w_up, rw_a0, rw_a_up, rw_g_up, rw_k_k, rw_k_a, rw_r_k, rw_ln_w, rw_ln_b, conv_w, w_o, w_ffn_in, ffn_conv_w, w_ffn_out):
    nb_p, s_len, d = x_prompt.shape
    nb_s, t_s, _ = x_sample.shape
    depth = w_in.shape[0]
    att = cache_k.shape[2] * HEAD_DIM
    n_heads = att // HEAD_DIM
    rdim = rw_w0.shape[1]
    rw_proj = rw_mu.shape[1]
    lora_ab = rw_w_up.shape[1] + rw_a_up.shape[1]
    lora_g = rw_g_up.shape[1]
    lora_g_pad = _round_up(lora_g, LANES)
    rw_pad = _round_up(3 * rdim + lora_ab + lora_g_pad, 512)
    lora_g_pad = rw_pad - 3 * rdim - lora_ab
    assert lora_ab == LANES
    past_len = page_table.shape[1] * PAGE_SIZE
    assert past_len % MOBA_BLOCK == 0 and s_len % MOBA_BLOCK == 0
    n_full = past_len // MOBA_BLOCK
    conv_dim = conv_w.shape[2]
    dff = ffn_conv_w.shape[2]

    xp = x_prompt.reshape(nb_p * s_len, d)
    xs = x_sample.reshape(nb_s * t_s, d)
    n_c = nb_p + nb_s
    c_all = jnp.concatenate([c_prompt, c_sample, jnp.zeros((_round_up(n_c, SUBLANES) - n_c, d), F32)], 0)

    cos_p, sin_p = _rope_tables(jnp.arange(s_len, dtype=jnp.int32))
    cos_s, sin_s = _rope_tables(past_len + jnp.arange(t_s, dtype=jnp.int32))

    tiles_p = dict(mm=_pick_tile(nb_p * s_len, 1024), mm_wide_k=_pick_tile(nb_p * s_len, 512),
                   ffn=_pick_tile(s_len, 1024), ew=MOBA_BLOCK, seq=MOBA_BLOCK, scan=MOBA_BLOCK // 4,
                   tiles_per_mod=s_len // MOBA_BLOCK)
    tiles_s = dict(mm=nb_s * t_s, mm_wide_k=nb_s * t_s, ffn=nb_s * t_s, ew=nb_s * t_s, seq=t_s, scan=t_s,
                   tiles_per_mod=1)

    zeros_p = (jnp.zeros((nb_p, 1, rw_pad), F32),
               jnp.zeros((nb_p, rdim // RWKV_HEAD_DIM, RWKV_HEAD_DIM, RWKV_HEAD_DIM), F32),
               jnp.zeros((nb_p, 2, conv_dim), F32), jnp.zeros((nb_p, 2, dff), F32))

    w_in_t = jnp.swapaxes(w_in, 1, 2)
    w_ffn_out_bf = w_ffn_out.astype(BF16)

    def layer_weights(l):
        pad_rows = lambda w, before, total: jnp.pad(w, ((before, total - before - w.shape[0]), (0, 0)))
        return dict(
            layer=l, att3=3 * att, rw_pad=rw_pad,
            w_in_t=w_in_t, w_cv_t=w_in_t[l, 3 * att + rw_proj:, :][None],
            w_o=w_o, w_ffn_in=w_ffn_in, w_ffn_out=w_ffn_out_bf,
            mu=jnp.pad(rw_mu[l], (0, rw_pad - rw_proj)), w0=rw_w0[l], a0=rw_a0[l],
            wup=pad_rows(rw_w_up[l], 0, lora_ab).astype(BF16),
            aup=pad_rows(rw_a_up[l], rw_w_up.shape[1], lora_ab).astype(BF16),
            gup=pad_rows(rw_g_up[l], 0, lora_g_pad).astype(BF16),
            k_k=rw_k_k[l], k_a=rw_k_a[l], r_k=rw_r_k[l], ln_w=rw_ln_w[l], ln_b=rw_ln_b[l],
            conv_w=conv_w[l], ffn_conv_w=ffn_conv_w[l],
            g_post_mix=g_post_mix[l], g_pre_ffn=g_pre_ffn[l],
            rdim=rdim, rw_proj=rw_proj, lora_ab=lora_ab)

    def attn_prompt(p_qkv):
        return _moba_prompt(p_qkv, nb_p, s_len, n_heads, cos_p, sin_p)

    def make_attn_sample(l):
        def fn(p_qkv):
            q_s, k_s, v_s = _rope_sample(p_qkv, nb_s, t_s, n_heads, cos_s, sin_s)
            means = _block_means(cache_k, l, page_table, n_full)
            idx = _topk_sample(q_s, means, n_full)
            idx_flat = idx[..., :MOBA_TOPK].reshape(-1)
            y = _attn_sample(q_s, k_s, v_s, cache_k, cache_v, l, idx_flat, page_table)
            return y.astype(BF16), k_s, v_s
        return fn

    outs_p, outs_s = [], []
    hp = hs = None
    fo_p = fo_s = gtf_p = gtf_s = None
    g_post_prev = None
    for l in range(depth):
        ada = _ada(c_all, w_ada, l, b_ada[l])
        mods_p = _mods(ada[:nb_p], d, False, s_len)
        mods_s = _mods(ada[nb_p:n_c], d, True, t_s)
        if l == 0:
            hp = _norm_mod(xp, g_pre_mix[l], mods_p[1], mods_p[0], tiles_p["ew"], tiles_p["tiles_per_mod"])
            hs = _norm_mod(xs, g_pre_mix[l], mods_s[1], mods_s[0], tiles_s["ew"], 1)
        else:
            xp, hp = _resid(fo_p, xp, g_post_prev, gtf_p, tiles_p["ew"], tiles_p["tiles_per_mod"],
                            nxt=(g_pre_mix[l], mods_p[1], mods_p[0]))
            xs, hs = _resid(fo_s, xs, g_post_prev, gtf_s, tiles_s["ew"], 1,
                            nxt=(g_pre_mix[l], mods_s[1], mods_s[0]))
        states_s = (jnp.pad(state_shift[l], ((0, 0), (0, rw_pad - rw_proj)))[:, None, :], state_wkv[l],
                    state_conv[l], state_ffn[l])
        main = dict(x=xp, h=hp, mods=mods_p, nseq=nb_p, t_len=s_len, attn=attn_prompt, states=zeros_p,
                    tiles=tiles_p)
        extra = dict(x=xs, h=hs, mods=mods_s, nseq=nb_s, t_len=t_s, attn=make_attn_sample(l),
                     states=states_s, tiles=tiles_s)
        (xp, fo_p, gtf_p, st_p), (xs, fo_s, gtf_s, st_s) = _layer(main, extra, layer_weights(l))
        g_post_prev = g_post_ffn[l]
        outs_p.append(st_p)
        outs_s.append(st_s)
    xp, _ = _resid(fo_p, xp, g_post_prev, gtf_p, tiles_p["ew"], tiles_p["tiles_per_mod"])
    xs, _ = _resid(fo_s, xs, g_post_prev, gtf_s, tiles_s["ew"], 1)

    stack = lambda sts, i: jnp.stack([st[i] for st in sts])
    return (xp.reshape(nb_p, s_len, d), xs.reshape(nb_s, t_s, d),
            stack(outs_p, 0), stack(outs_p, 1), stack(outs_s, 0), stack(outs_s, 1),
            stack(outs_p, 2), stack(outs_s, 2), stack(outs_p, 3), stack(outs_s, 3),
            stack(outs_p, 4), stack(outs_s, 4), stack(outs_p, 5), stack(outs_s, 5))
```

```python
import functools
import math

import jax
import jax.numpy as jnp
from jax import lax
from jax.experimental import pallas as pl
from jax.experimental.pallas import tpu as pltpu

F32 = jnp.float32
BF16 = jnp.bfloat16

LANES = 128
SUBLANES = 8
VMEM_LIMIT_BYTES = 56 * 1024 * 1024

HEAD_DIM = 128
RWKV_HEAD_DIM = 64
PAGE_SIZE = 128
MOBA_BLOCK = 256
MOBA_TOPK = 3
PAGES_PER_BLOCK = MOBA_BLOCK // PAGE_SIZE
ROPE_THETA = 10000.0
RMS_EPS = 1e-6
GN_EPS = 64e-5
NEG = -1e30
ATT_SCALE = HEAD_DIM ** -0.5

_NT = (((1,), (1,)), ((), ()))


def _cparams(*sem):
    return pltpu.CompilerParams(dimension_semantics=sem, vmem_limit_bytes=VMEM_LIMIT_BYTES)


def _round_up(x, m):
    return (x + m - 1) // m * m


def _pick_tile(n, pref):
    if n <= pref:
        return n
    t = pref
    while t >= LANES:
        if n % t == 0:
            return t
        t -= LANES
    return n


def _sigmoid(x):
    return 1.0 / (1.0 + jnp.exp(-x))


def _split_bf16(x):
    hi = x.astype(BF16)
    lo = (x - hi.astype(F32)).astype(BF16)
    return hi, lo


def _dot3_nt(a, b):
    ah, al = _split_bf16(a)
    bh, bl = _split_bf16(b)
    dg = functools.partial(lax.dot_general, dimension_numbers=_NT, preferred_element_type=F32)
    return dg(ah, bh) + dg(ah, bl) + dg(al, bh)


def _top3_lowest_index(gate, cand, lane):
    sel = jnp.zeros(gate.shape, F32)
    lane = lane.astype(F32)
    firsts = []
    for _ in range(MOBA_TOPK):
        gm = jnp.where(cand > 0.5, gate, -jnp.inf)
        mx = jnp.max(gm, axis=1, keepdims=True)
        is_max = jnp.where(cand > 0.5, jnp.where(gm == mx, 1.0, 0.0), 0.0)
        first = jnp.min(jnp.where(is_max > 0.5, lane, float(LANES)), axis=1, keepdims=True)
        pick = jnp.where(lane == first, 1.0, 0.0)
        sel = sel + pick
        cand = cand - pick
        firsts.append(first)
    return sel, firsts


def _ada_body(c_ref, w_ref, b_ref, o_ref):
    c = c_ref[...]
    s = c * _sigmoid(c)
    o_ref[...] = jnp.dot(s.astype(BF16), w_ref[...].astype(BF16),
                         preferred_element_type=F32) + b_ref[...]


def _ada(c_pad, w_all, layer, b):
    _, d, n = w_all.shape
    tn = _pick_tile(n, 512)
    return pl.pallas_call(
        _ada_body,
        grid=(n // tn,),
        in_specs=[pl.BlockSpec(c_pad.shape, lambda j: (0, 0)),
                  pl.BlockSpec((None, d, tn), lambda j: (layer, 0, j)),
                  pl.BlockSpec((1, tn), lambda j: (0, j))],
        out_specs=pl.BlockSpec((c_pad.shape[0], tn), lambda j: (0, j)),
        out_shape=jax.ShapeDtypeStruct((c_pad.shape[0], n), F32),
        compiler_params=_cparams("parallel"),
        name="ada",
    )(c_pad, w_all, b.reshape(1, n))


def _rms(x, g):
    ms = jnp.mean(x * x, axis=-1, keepdims=True)
    return x * lax.rsqrt(ms + RMS_EPS) * g


def _norm_mod_body(x_ref, g_ref, sc_ref, sh_ref, h_ref):
    h_ref[...] = (_rms(x_ref[...], g_ref[...]) * sc_ref[...] + sh_ref[...]).astype(h_ref.dtype)


def _mod_spec(mod, tiles_per_mod):
    return pl.BlockSpec((None,) + mod.shape[1:], lambda i: (i // tiles_per_mod, 0, 0))


def _norm_mod(x, g, sc1p, sh, tm, tiles_per_mod):
    m, d = x.shape
    row = pl.BlockSpec((tm, d), lambda i: (i, 0))
    return pl.pallas_call(
        _norm_mod_body,
        grid=(m // tm,),
        in_specs=[row, pl.BlockSpec((1, d), lambda i: (0, 0)),
                  _mod_spec(sc1p, tiles_per_mod), _mod_spec(sh, tiles_per_mod)],
        out_specs=row,
        out_shape=jax.ShapeDtypeStruct((m, d), BF16),
        compiler_params=_cparams("parallel"),
        name="norm_mod",
    )(x, g.reshape(1, d), sc1p, sh)


def _resid_body(*refs, with_h):
    if with_h:
        y_ref, x_ref, gp_ref, gate_ref, gn_ref, sc_ref, sh_ref, xo_ref, h_ref = refs
    else:
        y_ref, x_ref, gp_ref, gate_ref, xo_ref = refs
    xn = x_ref[...] + gate_ref[...] * _rms(y_ref[...], gp_ref[...])
    xo_ref[...] = xn
    if with_h:
        h_ref[...] = (_rms(xn, gn_ref[...]) * sc_ref[...] + sh_ref[...]).astype(h_ref.dtype)


def _resid(y, x, g_post, gate1p, tm, tiles_per_mod, nxt=None):
    m, d = x.shape
    row = pl.BlockSpec((tm, d), lambda i: (i, 0))
    vec = pl.BlockSpec((1, d), lambda i: (0, 0))
    ins = [y, x, g_post.reshape(1, d), gate1p]
    specs = [row, row, vec, _mod_spec(gate1p, tiles_per_mod)]
    out_shape = [jax.ShapeDtypeStruct((m, d), F32)]
    out_specs = [row]
    if nxt is not None:
        g_next, sc1p, sh = nxt
        ins += [g_next.reshape(1, d), sc1p, sh]
        specs += [vec, _mod_spec(sc1p, tiles_per_mod), _mod_spec(sh, tiles_per_mod)]
        out_shape.append(jax.ShapeDtypeStruct((m, d), BF16))
        out_specs.append(row)
    res = pl.pallas_call(
        functools.partial(_resid_body, with_h=nxt is not None),
        grid=(m // tm,),
        in_specs=specs,
        out_specs=out_specs,
        out_shape=out_shape,
        compiler_params=_cparams("parallel"),
        name="resid",
    )(*ins)
    return res if nxt is not None else (res[0], None)


def _mm_body(*refs, k_sizes, cast_w, w_is_t, with_extra):
    n_lhs = len(k_sizes)
    n_in = n_lhs * (2 if with_extra else 1)
    w_ref, o_ref = refs[n_in], refs[n_in + 1]
    scratch_at = n_in + (3 if with_extra else 2)
    if cast_w:
        wbf = refs[scratch_at]

        @pl.when(pl.program_id(1) == 0)
        def _():
            w = w_ref[...]
            wbf[...] = (w.T if w_is_t else w).astype(BF16)

        w_ref = wbf

    def product(lhs_refs):
        acc = None
        off = 0
        for a_ref, ks in zip(lhs_refs, k_sizes):
            part = jnp.dot(a_ref[...], w_ref[off:off + ks, :], preferred_element_type=F32)
            acc = part if acc is None else acc + part
            off += ks
        return acc

    o_ref[...] = product(refs[:n_lhs]).astype(o_ref.dtype)
    if with_extra:
        @pl.when(pl.program_id(1) == pl.num_programs(1) - 1)
        def _():
            refs[n_in + 2][...] = product(refs[n_lhs:n_in])


def _matmul(lhs_list, w_all, layer, col0, n, tm, tn, w_is_t=False, extra=None):
    m = lhs_list[0].shape[0]
    k_sizes = tuple(a.shape[1] for a in lhs_list)
    k = w_all.shape[2 if w_is_t else 1]
    assert sum(k_sizes) == k and m % tm == 0 and n % tn == 0 and col0 % tn == 0
    assert col0 + n <= w_all.shape[1 if w_is_t else 2]
    cast_w = w_all.dtype != BF16
    assert cast_w or not w_is_t
    j0 = col0 // tn
    w_spec = (pl.BlockSpec((None, tn, k), lambda j, i: (layer, j0 + j, 0)) if w_is_t
              else pl.BlockSpec((None, k, tn), lambda j, i: (layer, 0, j0 + j)))
    in_specs = [pl.BlockSpec((tm, ks), lambda j, i: (i, 0)) for ks in k_sizes]
    out_specs = [pl.BlockSpec((tm, tn), lambda j, i: (i, j))]
    out_shape = [jax.ShapeDtypeStruct((m, n), F32)]
    operands = list(lhs_list)
    if extra is not None:
        me = extra[0].shape[0]
        assert tuple(a.shape[1] for a in extra) == k_sizes
        in_specs += [pl.BlockSpec((me, ks), lambda j, i: (0, 0)) for ks in k_sizes]
        out_specs.append(pl.BlockSpec((me, tn), lambda j, i: (0, j)))
        out_shape.append(jax.ShapeDtypeStruct((me, n), F32))
        operands += list(extra)
    res = pl.pallas_call(
        functools.partial(_mm_body, k_sizes=k_sizes, cast_w=cast_w, w_is_t=w_is_t,
                          with_extra=extra is not None),
        grid=(n // tn, m // tm),
        in_specs=in_specs + [w_spec],
        out_specs=out_specs,
        out_shape=out_shape,
        scratch_shapes=[pltpu.VMEM((k, tn), BF16)] if cast_w else [],
        compiler_params=_cparams("parallel", "arbitrary"),
        name="matmul",
    )(*operands, w_all)
    return (res[0], res[1]) if extra is not None else (res[0], None)


def _shifted_rows(u, halo, state, first):
    p1 = jnp.where(first, state[1:2, :], halo[SUBLANES - 1:SUBLANES, :])
    p2 = jnp.where(first, state[0:1, :], halo[SUBLANES - 2:SUBLANES - 1, :])
    row = lax.broadcasted_iota(jnp.int32, u.shape, 0)
    u1 = jnp.where(row == 0, p1, pltpu.roll(u, 1, 0))
    u2 = jnp.where(row == 0, p2, jnp.where(row == 1, p1, pltpu.roll(u, 2, 0)))
    return u1, u2


def _shifted_rows_short(u, prev1, prev2, t_len):
    pos = lax.broadcasted_iota(jnp.int32, u.shape, 0) % t_len
    u1 = jnp.where(pos == 0, prev1, pltpu.roll(u, 1, 0))
    u2 = jnp.where(pos == 0, prev2, jnp.where(pos == 1, prev1, pltpu.roll(u, 2, 0)))
    return u1, u2


def _ffn_in_body(*refs, tiles_per_seq, t_len, t_len_extra=0):
    short = tiles_per_seq == 0
    if short:
        h_ref, wg_ref, wu_ref, p1_ref, p2_ref, cw_ref, f_ref, u_ref, wgb, wub = refs
    elif t_len_extra:
        (h_ref, wg_ref, wu_ref, s0_ref, cw_ref, he_ref, p1e_ref, p2e_ref,
         f_ref, s1_ref, fe_ref, ue_ref, wgb, wub, tail) = refs
    else:
        h_ref, wg_ref, wu_ref, s0_ref, cw_ref, f_ref, s1_ref, wgb, wub, tail = refs
    i = pl.program_id(1)

    @pl.when(i == 0)
    def _():
        wgb[...] = wg_ref[...].astype(BF16)
        wub[...] = wu_ref[...].astype(BF16)
        if not short:
            tail[...] = jnp.zeros(tail.shape, F32)

    h = h_ref[...]
    u = jnp.dot(h, wgb[...], preferred_element_type=F32)
    up = jnp.dot(h, wub[...], preferred_element_type=F32)
    if short:
        u1, u2 = _shifted_rows_short(u, p1_ref[...], p2_ref[...], t_len)
        u_ref[...] = u
    else:
        u1, u2 = _shifted_rows(u, tail[...], s0_ref[...], (i % tiles_per_seq) == 0)
        tail[...] = u[u.shape[0] - SUBLANES:, :]

        @pl.when((i % tiles_per_seq) == tiles_per_seq - 1)
        def _():
            s1_ref[...] = u[u.shape[0] - 2:, :]

    w = cw_ref[...]

    def gated(cur, prev1, prev2, up_rows):
        y = prev2 * w[0:1, :] + prev1 * w[1:2, :] + cur * w[2:3, :]
        return y * _sigmoid(y) * up_rows

    f_ref[...] = gated(u, u1, u2, up).astype(f_ref.dtype)
    if (not short) and t_len_extra:
        @pl.when(i == pl.num_programs(1) - 1)
        def _():
            he = he_ref[...]
            ue = jnp.dot(he, wgb[...], preferred_element_type=F32)
            upe = jnp.dot(he, wub[...], preferred_element_type=F32)
            e1, e2 = _shifted_rows_short(ue, p1e_ref[...], p2e_ref[...], t_len_extra)
            ue_ref[...] = ue
            fe_ref[...] = gated(ue, e1, e2, upe).astype(fe_ref.dtype)


def _ffn_in(h, w_all, layer, ffn0, ffn_w, tm, tc, t_len, extra=None):
    m, k = h.shape
    dff = ffn_w.shape[1]
    nseq = ffn0.shape[0]
    ncol = dff // tc
    assert dff % tc == 0 and m % tm == 0
    short = tm > t_len
    tiles_per_seq = 0 if short else t_len // tm
    lhs = pl.BlockSpec((tm, k), lambda j, i: (i, 0))
    wspec = lambda off: pl.BlockSpec((None, k, tc), lambda j, i: (layer, 0, off + j))
    tile = pl.BlockSpec((tm, tc), lambda j, i: (i, j))
    cw = pl.BlockSpec((3, tc), lambda j, i: (0, j))
    scratch = [pltpu.VMEM((k, tc), BF16), pltpu.VMEM((k, tc), BF16)]
    if short:
        assert tm == m and m == nseq * t_len
        prev1 = jnp.repeat(ffn0[:, 1, :], t_len, axis=0)
        prev2 = jnp.repeat(ffn0[:, 0, :], t_len, axis=0)
        f, u = pl.pallas_call(
            functools.partial(_ffn_in_body, tiles_per_seq=0, t_len=t_len),
            grid=(ncol, 1),
            in_specs=[lhs, wspec(0), wspec(ncol), tile, tile, cw],
            out_specs=[tile, tile],
            out_shape=[jax.ShapeDtypeStruct((m, dff), BF16), jax.ShapeDtypeStruct((m, dff), F32)],
            scratch_shapes=scratch,
            compiler_params=_cparams("parallel", "arbitrary"),
            name="ffn_in",
        )(h, w_all, w_all, prev1, prev2, ffn_w)
        return f, u.reshape(nseq, t_len, dff)[:, t_len - 2:, :]
    state = pl.BlockSpec((None, 2, tc), lambda j, i: (i // tiles_per_seq, 0, j))
    in_specs = [lhs, wspec(0), wspec(ncol), state, cw]
    out_specs = [tile, state]
    out_shape = [jax.ShapeDtypeStruct((m, dff), BF16), jax.ShapeDtypeStruct((nseq, 2, dff), F32)]
    operands = [h, w_all, w_all, ffn0, ffn_w]
    t_len_e = 0
    if extra is not None:
        he, ffn0_e, t_len_e = extra
        me = he.shape[0]
        etile = pl.BlockSpec((me, tc), lambda j, i: (0, j))
        in_specs += [pl.BlockSpec((me, k), lambda j, i: (0, 0)), etile, etile]
        out_specs += [etile, etile]
        out_shape += [jax.ShapeDtypeStruct((me, dff), BF16), jax.ShapeDtypeStruct((me, dff), F32)]
        operands += [he, jnp.repeat(ffn0_e[:, 1, :], t_len_e, axis=0), jnp.repeat(ffn0_e[:, 0, :], t_len_e, axis=0)]
    res = pl.pallas_call(
        functools.partial(_ffn_in_body, tiles_per_seq=tiles_per_seq, t_len=t_len, t_len_extra=t_len_e),
        grid=(ncol, m // tm),
        in_specs=in_specs,
        out_specs=out_specs,
        out_shape=out_shape,
        scratch_shapes=scratch + [pltpu.VMEM((SUBLANES, tc), F32)],
        compiler_params=_cparams("parallel", "arbitrary"),
        name="ffn_in",
    )(*operands)
    if extra is None:
        return res[0], res[1]
    n_e = ffn0_e.shape[0]
    return res[0], res[1], res[2], res[3].reshape(n_e, t_len_e, dff)[:, t_len_e - 2:, :]


def _rope_tables(pos):
    half = HEAD_DIM // 2
    inv = ROPE_THETA ** (-jnp.arange(half, dtype=F32) / half)
    ang = pos.astype(F32)[:, None] * inv[None, :]
    cos, sin = jnp.cos(ang), jnp.sin(ang)
    return jnp.concatenate([cos, cos], axis=1), jnp.concatenate([-sin, sin], axis=1)


def _rope(x, cos2, sin2):
    return x * cos2 + pltpu.roll(x, HEAD_DIM // 2, 1) * sin2


def _top3_rows(gate, cand, blk):
    sel = jnp.zeros(gate.shape, F32)
    for _ in range(MOBA_TOPK):
        gm = jnp.where(cand > 0.5, gate, -jnp.inf)
        mx = jnp.max(gm, axis=0, keepdims=True)
        is_max = jnp.where(cand > 0.5, jnp.where(gm == mx, 1.0, 0.0), 0.0)
        first = jnp.min(jnp.where(is_max > 0.5, blk, float(LANES)), axis=0, keepdims=True)
        pick = jnp.where(blk == first, 1.0, 0.0)
        sel = sel + pick
        cand = cand - pick
    return sel


def _moba_pt_body(q_ref, k_ref, v_ref, cos_ref, sin_ref, y_ref, ko_ref, vo_ref, kbf, vtbf, means):
    s_len = k_ref.shape[0]
    nb = s_len // MOBA_BLOCK
    n_pages = s_len // PAGE_SIZE
    nbp = means.shape[0]

    kr = _rope(k_ref[...], cos_ref[...], sin_ref[...])
    v = v_ref[...]
    ko_ref[...] = kr.reshape(n_pages, PAGE_SIZE, HEAD_DIM)
    vo_ref[...] = v.reshape(n_pages, PAGE_SIZE, HEAD_DIM)
    kbf[...] = kr.astype(BF16)
    vtbf[...] = v.T.astype(BF16)
    means[...] = jnp.zeros(means.shape, F32)
    means[0:nb, :] = jnp.mean(kr.reshape(nb, MOBA_BLOCK, HEAD_DIM), axis=1)
    mh, ml = _split_bf16(means[...])

    blk = lax.broadcasted_iota(jnp.int32, (nbp, MOBA_BLOCK), 0).astype(F32)
    key = lax.broadcasted_iota(jnp.int32, (MOBA_BLOCK, MOBA_BLOCK), 0)
    qry = lax.broadcasted_iota(jnp.int32, (MOBA_BLOCK, MOBA_BLOCK), 1)
    for n in range(nb):
        rows = slice(n * MOBA_BLOCK, (n + 1) * MOBA_BLOCK)
        q_t = _rope(q_ref[rows, :], cos_ref[rows, :], sin_ref[rows, :]).T
        qh, ql = _split_bf16(q_t)
        end = (n + 1) * MOBA_BLOCK
        s_t = jnp.dot(kbf[0:end, :], qh, preferred_element_type=F32) * ATT_SCALE
        pieces = [jnp.where(key <= qry, s_t[n * MOBA_BLOCK:end, :], NEG)]
        if n > 0:
            dot = functools.partial(jnp.dot, preferred_element_type=F32)
            gate = dot(mh, qh) + dot(mh, ql) + dot(ml, qh)
            sel = _top3_rows(gate, jnp.where(blk < n, 1.0, 0.0), blk)
            pieces = [jnp.where(sel[j:j + 1, :] > 0.5, s_t[j * MOBA_BLOCK:(j + 1) * MOBA_BLOCK, :], NEG)
                      for j in range(n)] + pieces
        mx = pieces[0].max(axis=0, keepdims=True)
        for s in pieces[1:]:
            mx = jnp.maximum(mx, s.max(axis=0, keepdims=True))
        p = [jnp.exp(s - mx) for s in pieces]
        l = p[0].sum(axis=0, keepdims=True)
        for x in p[1:]:
            l = l + x.sum(axis=0, keepdims=True)
        p_t = jnp.concatenate([x.astype(BF16) for x in p], axis=0)
        o_t = jnp.dot(vtbf[:, 0:end], p_t, preferred_element_type=F32) / l
        y_ref[rows, :] = o_t.T.astype(y_ref.dtype)


def _moba_prompt(p_qkv, nseq, s_len, n_heads, cos2, sin2):
    n_pages = s_len // PAGE_SIZE
    m = nseq * s_len
    nbp = _round_up(s_len // MOBA_BLOCK, 2 * SUBLANES)
    full = lambda off: pl.BlockSpec((s_len, HEAD_DIM), lambda b, h: (b, off + h))
    tab = pl.BlockSpec((s_len, HEAD_DIM), lambda b, h: (0, 0))
    pages = pl.BlockSpec((None, n_pages, None, PAGE_SIZE, HEAD_DIM), lambda b, h: (b, 0, h, 0, 0))
    page_shape = jax.ShapeDtypeStruct((nseq, n_pages, n_heads, PAGE_SIZE, HEAD_DIM), F32)
    return pl.pallas_call(
        _moba_pt_body,
        grid=(nseq, n_heads),
        in_specs=[full(0), full(n_heads), full(2 * n_heads), tab, tab],
        out_specs=[full(0), pages, pages],
        out_shape=[jax.ShapeDtypeStruct((m, n_heads * HEAD_DIM), BF16), page_shape, page_shape],
        scratch_shapes=[pltpu.VMEM((s_len, HEAD_DIM), BF16), pltpu.VMEM((HEAD_DIM, s_len), BF16),
                        pltpu.VMEM((nbp, HEAD_DIM), F32)],
        compiler_params=_cparams("parallel", "parallel"),
        name="moba_prompt",
    )(p_qkv, p_qkv, p_qkv, cos2, sin2)


def _rope_s_body(p_ref, cos_ref, sin_ref, q_o, k_o, v_o, *, n_heads):
    cos2, sin2 = cos_ref[...], sin_ref[...]
    for h in range(n_heads):
        sl = lambda part: p_ref[:, (part * n_heads + h) * HEAD_DIM:(part * n_heads + h + 1) * HEAD_DIM]
        q_o[h] = _rope(sl(0), cos2, sin2)
        k_o[h] = _rope(sl(1), cos2, sin2)
        v_o[h] = sl(2)


def _rope_sample(p_qkv, nseq, t_len, n_heads, cos2, sin2):
    shp = jax.ShapeDtypeStruct((nseq, n_heads, t_len, HEAD_DIM), F32)
    out = pl.BlockSpec((None, n_heads, t_len, HEAD_DIM), lambda b: (b, 0, 0, 0))
    tab = pl.BlockSpec((t_len, HEAD_DIM), lambda b: (0, 0))
    return pl.pallas_call(
        functools.partial(_rope_s_body, n_heads=n_heads),
        grid=(nseq,),
        in_specs=[pl.BlockSpec((t_len, p_qkv.shape[1]), lambda b: (b, 0)), tab, tab],
        out_specs=[out, out, out],
        out_shape=[shp, shp, shp],
        compiler_params=_cparams("parallel"),
        name="rope_sample",
    )(p_qkv, cos2, sin2)


_MEANS_BLOCKS_PER_STEP = 4


def _means_body(pt_ref, *refs, n_heads):
    page_refs, o_ref = refs[:-1], refs[-1]
    j = pl.program_id(1)

    @pl.when(j == 0)
    def _():
        o_ref[...] = jnp.zeros(o_ref.shape, F32)

    for blk in range(_MEANS_BLOCKS_PER_STEP):
        s = jnp.sum(page_refs[PAGES_PER_BLOCK * blk][...], axis=1)
        for pg in range(1, PAGES_PER_BLOCK):
            s = s + jnp.sum(page_refs[PAGES_PER_BLOCK * blk + pg][...], axis=1)
        s = s * (1.0 / MOBA_BLOCK)
        for h in range(n_heads):
            o_ref[h, pl.ds(j * _MEANS_BLOCKS_PER_STEP + blk, 1), :] = s[h:h + 1, :]


def _block_means(cache_k, layer, page_table, n_full):
    nseq = page_table.shape[0]
    n_heads = cache_k.shape[2]
    pages_per_step = PAGES_PER_BLOCK * _MEANS_BLOCKS_PER_STEP
    assert n_full <= LANES and n_full % _MEANS_BLOCKS_PER_STEP == 0
    page = lambda pg: pl.BlockSpec(
        (None, None, n_heads, PAGE_SIZE, HEAD_DIM),
        lambda b, j, pt: (layer, pt[b, pages_per_step * j + pg], 0, 0, 0))
    return pl.pallas_call(
        functools.partial(_means_body, n_heads=n_heads),
        grid_spec=pltpu.PrefetchScalarGridSpec(
            num_scalar_prefetch=1,
            grid=(nseq, n_full // _MEANS_BLOCKS_PER_STEP),
            in_specs=[page(pg) for pg in range(pages_per_step)],
            out_specs=pl.BlockSpec((None, n_heads, LANES, HEAD_DIM), lambda b, j, pt: (b, 0, 0, 0)),
        ),
        out_shape=jax.ShapeDtypeStruct((nseq, n_heads, LANES, HEAD_DIM), F32),
        compiler_params=_cparams("parallel", "arbitrary"),
        name="block_means",
    )(page_table, *([cache_k] * pages_per_step))


def _topk_s_body(q_ref, m_ref, o_ref, *, n_heads, n_full):
    for h in range(n_heads):
        gate = _dot3_nt(q_ref[h], m_ref[h])
        lane = lax.broadcasted_iota(jnp.int32, gate.shape, 1)
        cand = jnp.where(lane < n_full, 1.0, 0.0)
        _, firsts = _top3_lowest_index(gate, cand, lane)
        idx = jnp.zeros(gate.shape, F32)
        for r, first in enumerate(firsts):
            idx = jnp.where(lane == r, first, idx)
        o_ref[h] = idx.astype(jnp.int32)


def _topk_sample(q_s, means, n_full):
    nseq, n_heads, t_len, _ = q_s.shape
    return pl.pallas_call(
        functools.partial(_topk_s_body, n_heads=n_heads, n_full=n_full),
        grid=(nseq,),
        in_specs=[pl.BlockSpec((None, n_heads, t_len, HEAD_DIM), lambda b: (b, 0, 0, 0)),
                  pl.BlockSpec((None, n_heads, LANES, HEAD_DIM), lambda b: (b, 0, 0, 0))],
        out_specs=pl.BlockSpec((None, n_heads, t_len, LANES), lambda b: (b, 0, 0, 0)),
        out_shape=jax.ShapeDtypeStruct((nseq, n_heads, t_len, LANES), jnp.int32),
        compiler_params=_cparams("parallel"),
        name="topk_sample",
    )(q_s, means)


_N_SEL_PAGES = MOBA_TOPK * PAGES_PER_BLOCK


def _attn_s_body(idx_ref, pt_ref, q_ref, kn_ref, vn_ref, *refs):
    q = q_ref[...]
    t_len = q.shape[0]
    n_pg = t_len * _N_SEL_PAGES
    k_refs, v_refs, o_ref = refs[:n_pg], refs[n_pg:2 * n_pg], refs[-1]
    kn, vn = kn_ref[...], vn_ref[...]
    key_new = lax.broadcasted_iota(jnp.int32, (t_len, 1), 0)
    outs = []
    for t in range(t_len):
        q_t = q[t:t + 1, :]
        pages = slice(t * _N_SEL_PAGES, (t + 1) * _N_SEL_PAGES)
        score = lambda keys: jnp.sum(keys * q_t, axis=1, keepdims=True) * ATT_SCALE
        s_all = [score(k[...]) for k in k_refs[pages]]
        s_all.append(jnp.where(key_new <= t, score(kn), NEG))
        mx = jnp.max(s_all[0], axis=0, keepdims=True)
        for s in s_all[1:]:
            mx = jnp.maximum(mx, jnp.max(s, axis=0, keepdims=True))
        l = jnp.zeros((1, 1), F32)
        o = jnp.zeros((1, HEAD_DIM), F32)
        for s, v in zip(s_all, [v[...] for v in v_refs[pages]] + [vn]):
            p = jnp.exp(s - mx)
            l = l + jnp.sum(p, axis=0, keepdims=True)
            o = o + jnp.sum(p * v, axis=0, keepdims=True)
        outs.append(o / l)
    o_ref[...] = jnp.concatenate(outs, axis=0)


def _attn_sample(q_s, k_new, v_new, cache_k, cache_v, layer, idx_flat, page_table):
    nseq, n_heads, t_len, _ = q_s.shape

    def page(t, s, pg):
        def imap(b, h, idx, pt):
            blk = idx[((b * n_heads + h) * t_len + t) * MOBA_TOPK + s]
            return (layer, pt[b, blk * PAGES_PER_BLOCK + pg], h, 0, 0)
        return pl.BlockSpec((None, None, None, PAGE_SIZE, HEAD_DIM), imap)

    pages = [page(t, s, pg) for t in range(t_len) for s in range(MOBA_TOPK) for pg in range(PAGES_PER_BLOCK)]
    new = pl.BlockSpec((None, None, t_len, HEAD_DIM), lambda b, h, idx, pt: (b, h, 0, 0))
    return pl.pallas_call(
        _attn_s_body,
        grid_spec=pltpu.PrefetchScalarGridSpec(
            num_scalar_prefetch=2,
            grid=(nseq, n_heads),
            in_specs=[new, new, new] + pages + pages,
            out_specs=pl.BlockSpec((t_len, HEAD_DIM), lambda b, h, idx, pt: (b, h)),
        ),
        out_shape=jax.ShapeDtypeStruct((nseq * t_len, n_heads * HEAD_DIM), F32),
        compiler_params=_cparams("parallel", "parallel"),
        name="attn_sample",
    )(idx_flat, page_table, q_s, k_new, v_new, *([cache_k] * len(pages)), *([cache_v] * len(pages)))


def _rw_prep_body(p_ref, halo_ref, sp_ref, mu_ref, w0_ref, wup_ref, a0_ref, aup_ref, gup_ref,
                  r_o, k_o, v_o, a_o, d_o, g_o, *, tiles_per_seq, rdim, lora_ab):
    i = pl.program_id(0)
    pf = p_ref[...]
    first = (i % tiles_per_seq) == 0
    prev0 = jnp.where(first, sp_ref[...], halo_ref[SUBLANES - 1:SUBLANES, :])
    row = lax.broadcasted_iota(jnp.int32, pf.shape, 0)
    prev = jnp.where(row == 0, prev0, pltpu.roll(pf, 1, 0))
    pm = pf + (prev - pf) * mu_ref[...]
    r, k, v = pm[:, 0:rdim], pm[:, rdim:2 * rdim], pm[:, 2 * rdim:3 * rdim]
    ca = pm[:, 3 * rdim:3 * rdim + lora_ab]
    cb = pm[:, 3 * rdim + lora_ab:]
    w_lin = jnp.dot(jnp.tanh(ca).astype(BF16), wup_ref[...], preferred_element_type=F32)
    a_lin = jnp.dot(ca.astype(BF16), aup_ref[...], preferred_element_type=F32)
    g = jnp.dot(_sigmoid(cb).astype(BF16), gup_ref[...], preferred_element_type=F32)
    y = -(w0_ref[...] + w_lin)
    w = -(jnp.maximum(y, 0.0) + jnp.log(1.0 + jnp.exp(-jnp.abs(y)))) - 0.5
    a = _sigmoid(a0_ref[...] + a_lin)
    r_o[...] = r
    k_o[...] = k
    v_o[...] = v
    a_o[...] = a
    d_o[...] = jnp.exp(-jnp.exp(w))
    g_o[...] = g


def _rw_prep(p_rw, shift_prev, mu, w0, wup, a0, aup, gup, tm, tiles_per_seq, rdim, lora_ab):
    m, wp = p_rw.shape
    row = pl.BlockSpec((tm, wp), lambda i: (i, 0))
    halo = pl.BlockSpec((SUBLANES, wp), lambda i: (jnp.maximum(i * (tm // SUBLANES) - 1, 0), 0))
    const = lambda a: pl.BlockSpec(a.shape, lambda i: (0,) * a.ndim)
    out = pl.BlockSpec((tm, rdim), lambda i: (i, 0))
    shp = jax.ShapeDtypeStruct((m, rdim), F32)
    vecs = [x.reshape(1, -1) for x in (mu, w0)]
    return pl.pallas_call(
        functools.partial(_rw_prep_body, tiles_per_seq=tiles_per_seq, rdim=rdim, lora_ab=lora_ab),
        grid=(m // tm,),
        in_specs=[row, halo, pl.BlockSpec((None, 1, wp), lambda i: (i // tiles_per_seq, 0, 0)),
                  const(vecs[0]), const(vecs[1]), const(wup), const(a0.reshape(1, -1)), const(aup),
                  const(gup)],
        out_specs=[out] * 6,
        out_shape=[shp] * 6,
        compiler_params=_cparams("parallel"),
        name="rwkv_prep",
    )(p_rw, p_rw, shift_prev, vecs[0], vecs[1], wup, a0.reshape(1, -1), aup, gup)


_SCAN_ROWS_IN_FLIGHT = 4


def _rw_scan_body(r_ref, k_ref, a_ref, d_ref, v_ref, kkt_ref, kat_ref, rk_ref, lnw_ref, lnb_ref, s0_ref,
                  o_ref, so_ref, state, b_s, c_s, kp_s, d_s, r_s, bd_s, al_s, be_s, rk_s, o_s,
                  *, halves, chains, packed):
    pid = pl.program_id(0)
    tc, nv = v_ref.shape[0], v_ref.shape[1]
    n_val = nv * halves
    rows = min(_SCAN_ROWS_IN_FLIGHT, nv)

    @pl.when(pid == 0)
    def _():
        state[...] = s0_ref[...]

    np_ = tc // 2

    def even_odd(ref):
        x = ref[...]
        if packed:
            xr = pltpu.roll(x.reshape(np_ * x.shape[1], LANES), LANES // 2, 1).reshape(x.shape)
            low = lax.broadcasted_iota(jnp.int32, x.shape, 2) < LANES // 2
            return jnp.where(low, x, xr), jnp.where(low, xr, x)
        x = x.reshape((np_, 2) + x.shape[1:])
        return x[:, 0], x[:, 1]

    k2, a2, d2, r2 = even_odd(k_ref), even_odd(a_ref), even_odd(d_ref), even_odd(r_ref)
    b2, c2, kp2 = [], [], []
    for par in range(2):
        k, a = k2[par], a2[par]
        kk = k * kkt_ref[...]
        kkn = kk / jnp.maximum(jnp.sqrt(jnp.sum(kk * kk, axis=1, keepdims=True)), 1e-12)
        kp = k * (1.0 + (a - 1.0) * kat_ref[...])
        b2.append(-kkn)
        c2.append(kkn * a)
        kp2.append(kp)
        b_s[par] = b2[par]
        c_s[par] = c2[par]
        kp_s[par] = kp
        d_s[par] = d2[par]
        r_s[par] = r2[par]
        rk_s[par] = jnp.sum(r2[par] * kp * rk_ref[...], axis=1, keepdims=True)
    bd_s[...] = d2[0] * b2[1]
    al_s[...] = jnp.sum(c2[0] * b2[1], axis=1, keepdims=True)
    be_s[...] = jnp.sum(kp2[0] * b2[1], axis=1, keepdims=True)

    for g in range(nv // rows):
        def step(s, s_rows, g=g):
            t0, t1 = 2 * s, 2 * s + 1
            b0, bd, al, be = b_s[0, s], bd_s[s], al_s[s], be_s[s]
            c0, kp0, d0, r0 = c_s[0, s], kp_s[0, s], d_s[0, s], r_s[0, s]
            c1, kp1, d1, r1 = c_s[1, s], kp_s[1, s], d_s[1, s], r_s[1, s]
            new = []
            for i, sv in enumerate(s_rows):
                vi = g * rows + i
                v0, v1 = v_ref[t0, vi:vi + 1, :], v_ref[t1, vi:vi + 1, :]
                u0 = jnp.sum(sv * b0, axis=0, keepdims=True)
                u1 = jnp.sum(sv * bd, axis=0, keepdims=True) + u0 * al + v0 * be
                s1 = sv * d0 + u0 * c0 + v0 * kp0
                o_s[t0, vi:vi + 1, :] = jnp.sum(s1 * r0, axis=0, keepdims=True)
                s2 = s1 * d1 + u1 * c1 + v1 * kp1
                o_s[t1, vi:vi + 1, :] = jnp.sum(s2 * r1, axis=0, keepdims=True)
                new.append(s2)
            return tuple(new)

        s_rows = lax.fori_loop(0, tc // 2, step, tuple(state[g * rows + i] for i in range(rows)))
        for i, sv in enumerate(s_rows):
            state[g * rows + i] = sv

    def head_sum(x):
        x = x.reshape(tc * nv, LANES)
        shift = chains
        for _ in range(int(math.log2(halves))):
            x = x + pltpu.roll(x, shift, 1)
            shift *= 2
        return jnp.sum(x.reshape(tc, nv, LANES), axis=1, keepdims=True)

    o = o_s[...]
    dev = o - head_sum(o) * (1.0 / n_val)
    var = head_sum(dev * dev) * (1.0 / n_val)
    on = dev * lax.rsqrt(var + GN_EPS) * lnw_ref[...] + lnb_ref[...]
    rk = jnp.stack([rk_s[0], rk_s[1]], axis=1).reshape(tc, 1, LANES)
    o_ref[...] = on + rk * v_ref[...]

    @pl.when(pid == pl.num_programs(0) - 1)
    def _():
        so_ref[...] = state[...]


def _rw_scan(r, k, a, d, v, kk_tile, ka_tile, rk_tile, lnw_tile, lnb_tile, s0, tc, halves, chains, packed):
    t_len, nv = v.shape[0], v.shape[1]
    kdim = r.shape[1]
    assert nv % min(_SCAN_ROWS_IN_FLIGHT, nv) == 0 and t_len % tc == 0 and tc % 2 == 0
    np_ = tc // 2
    ktile = pl.BlockSpec((np_ if packed else tc, kdim, LANES), lambda i: (i, 0, 0))
    vtile = pl.BlockSpec((tc, nv, LANES), lambda i: (i, 0, 0))
    const = lambda x: pl.BlockSpec(x.shape, lambda i: (0,) * x.ndim)
    consts = [kk_tile, ka_tile, rk_tile, lnw_tile, lnb_tile, s0]
    return pl.pallas_call(
        functools.partial(_rw_scan_body, halves=halves, chains=chains, packed=packed),
        grid=(t_len // tc,),
        in_specs=[ktile] * 4 + [vtile] + [const(x) for x in consts],
        out_specs=[vtile, const(s0)],
        out_shape=[jax.ShapeDtypeStruct(v.shape, F32), jax.ShapeDtypeStruct(s0.shape, F32)],
        scratch_shapes=[pltpu.VMEM(s0.shape, F32)] + [pltpu.VMEM((2, np_, kdim, LANES), F32)] * 5
        + [pltpu.VMEM((np_, kdim, LANES), F32)] + [pltpu.VMEM((np_, 1, LANES), F32)] * 2
        + [pltpu.VMEM((2, np_, 1, LANES), F32), pltpu.VMEM((tc, nv, LANES), F32)],
        compiler_params=_cparams("arbitrary"),
        name="rwkv_scan",
    )(r, k, a, d, v, *consts)


def _conv_mix_body(gb_ref, gc_ref, hc_ref, gch_ref, hch_ref, c0_ref, w_ref, o_ref, g_ref,
                   ycv_ref, yrw_ref, c1_ref, *, tiles_per_seq):
    i = pl.program_id(0)
    first = (i % tiles_per_seq) == 0
    u = gc_ref[...] * hc_ref[...]
    u1, u2 = _shifted_rows(u, gch_ref[...] * hch_ref[...], c0_ref[...], first)
    w = w_ref[...]
    y = u2 * w[0:1, :] + u1 * w[1:2, :] + u * w[2:3, :]
    ycv_ref[...] = (gb_ref[...] * y).astype(ycv_ref.dtype)
    yrw_ref[...] = (o_ref[...] * g_ref[...]).astype(yrw_ref.dtype)

    @pl.when((i % tiles_per_seq) == tiles_per_seq - 1)
    def _():
        c1_ref[...] = u[u.shape[0] - 2:, :]


def _conv_mix(p_cv, conv0, conv_w, o_rw, g_rw, tm, tiles_per_seq):
    m = p_cv.shape[0]
    c = conv_w.shape[1]
    nseq = conv0.shape[0]
    col = lambda j: pl.BlockSpec((tm, c), lambda i: (i, j))
    halo = lambda j: pl.BlockSpec((SUBLANES, c), lambda i: (jnp.maximum(i * (tm // SUBLANES) - 1, 0), j))
    state = pl.BlockSpec((None, 2, c), lambda i: (i // tiles_per_seq, 0, 0))
    rw = pl.BlockSpec((tm, o_rw.shape[1]), lambda i: (i, 0))
    return pl.pallas_call(
        functools.partial(_conv_mix_body, tiles_per_seq=tiles_per_seq),
        grid=(m // tm,),
        in_specs=[col(0), col(1), col(2), halo(1), halo(2), state,
                  pl.BlockSpec((3, c), lambda i: (0, 0)), rw, rw],
        out_specs=[col(0), rw, state],
        out_shape=[jax.ShapeDtypeStruct((m, c), BF16), jax.ShapeDtypeStruct(o_rw.shape, BF16),
                   jax.ShapeDtypeStruct((nseq, 2, c), F32)],
        compiler_params=_cparams("arbitrary"),
        name="conv_mix",
    )(p_cv, p_cv, p_cv, p_cv, p_cv, conv0, conv_w, o_rw, g_rw)


def _chain_geometry(nseq, n_heads):
    chains = nseq * n_heads
    assert LANES % chains == 0
    halves = LANES // chains
    assert RWKV_HEAD_DIM % halves == 0
    return chains, halves, RWKV_HEAD_DIM // halves


def _to_chain_k(x, nseq, t_len, n_heads, halves):
    if halves == 2:
        x = x.reshape(nseq, t_len // 2, 2, n_heads, RWKV_HEAD_DIM).transpose(1, 4, 2, 0, 3)
        return x.reshape(t_len // 2, RWKV_HEAD_DIM, LANES)
    x = x.reshape(nseq, t_len, n_heads, RWKV_HEAD_DIM).transpose(1, 3, 0, 2)
    return jnp.tile(x.reshape(t_len, RWKV_HEAD_DIM, nseq * n_heads), (1, 1, halves))


def _to_chain_v(x, nseq, t_len, n_heads, halves, nv):
    x = x.reshape(nseq, t_len, n_heads, halves, nv).transpose(1, 4, 3, 0, 2)
    return x.reshape(t_len, nv, LANES)


def _from_chain_v(x, nseq, t_len, n_heads, halves, nv):
    x = x.reshape(t_len, nv, halves, nseq, n_heads).transpose(3, 0, 4, 2, 1)
    return x.reshape(nseq * t_len, n_heads * RWKV_HEAD_DIM)


def _state_to_chain(s, nseq, n_heads, halves, nv):
    s = s.reshape(nseq, n_heads, halves, nv, RWKV_HEAD_DIM).transpose(3, 4, 2, 0, 1)
    return s.reshape(nv, RWKV_HEAD_DIM, LANES)


def _state_from_chain(s, nseq, n_heads, halves, nv):
    s = s.reshape(nv, RWKV_HEAD_DIM, halves, nseq, n_heads).transpose(3, 4, 2, 0, 1)
    return s.reshape(nseq, n_heads, RWKV_HEAD_DIM, RWKV_HEAD_DIM)


def _head_tile_k(x, nseq, halves):
    return jnp.tile(x.T, (1, halves * nseq))


def _head_tile_v(x, nseq, n_heads, halves, nv):
    x = x.reshape(n_heads, halves, nv).transpose(2, 1, 0)
    return jnp.broadcast_to(x[:, :, None, :], (nv, halves, nseq, n_heads)).reshape(nv, LANES)


def _mods(ada_rows, d, per_row, rows_per_seq):
    parts = [ada_rows[:, i * d:(i + 1) * d] for i in range(6)]
    sh_m, sc_m, gt_m, sh_f, sc_f, gt_f = parts
    out = (sh_m, 1.0 + sc_m, 1.0 + gt_m, sh_f, 1.0 + sc_f, 1.0 + gt_f)
    if per_row:
        return tuple(jnp.repeat(z, rows_per_seq, axis=0)[None] for z in out)
    return tuple(z[:, None, :] for z in out)


def _mixers(p_qkv, p_rw, p_cv, nseq, t_len, lw, attn_fn, states, tiles):
    shift0, wkv0, conv0, _ = states
    tm_seq = tiles["seq"]
    tiles_per_seq = t_len // tm_seq
    rdim = lw["rdim"]
    n_rw_heads = rdim // RWKV_HEAD_DIM

    y_att, k_new, v_new = attn_fn(p_qkv)

    r, k, v, a, dec, g = _rw_prep(p_rw, shift0, lw["mu"], lw["w0"], lw["wup"], lw["a0"], lw["aup"],
                                  lw["gup"], tm_seq, tiles_per_seq, rdim, lw["lora_ab"])
    chains, halves, nv = _chain_geometry(nseq, n_rw_heads)
    ck = lambda z: _to_chain_k(z, nseq, t_len, n_rw_heads, halves)
    per_head = lambda z: z.reshape(n_rw_heads, RWKV_HEAD_DIM)
    o_c, s_c = _rw_scan(ck(r), ck(k), ck(a), ck(dec),
                        _to_chain_v(v, nseq, t_len, n_rw_heads, halves, nv),
                        _head_tile_k(per_head(lw["k_k"]), nseq, halves),
                        _head_tile_k(per_head(lw["k_a"]), nseq, halves),
                        _head_tile_k(lw["r_k"], nseq, halves),
                        _head_tile_v(lw["ln_w"], nseq, n_rw_heads, halves, nv),
                        _head_tile_v(lw["ln_b"], nseq, n_rw_heads, halves, nv),
                        _state_to_chain(wkv0, nseq, n_rw_heads, halves, nv),
                        tiles["scan"], halves, chains, halves == 2)
    o_rw = _from_chain_v(o_c, nseq, t_len, n_rw_heads, halves, nv)
    wkv1 = _state_from_chain(s_c, nseq, n_rw_heads, halves, nv)
    shift1 = p_rw.reshape(nseq, t_len, -1)[:, -1, :lw["rw_proj"]]

    y_cv, y_rw, conv1 = _conv_mix(p_cv, conv0, lw["conv_w"], o_rw, g, tm_seq, tiles_per_seq)
    return [y_att, y_rw, y_cv], (k_new, v_new, wkv1, shift1, conv1)


def _layer(main, extra, lw):
    d = main["x"].shape[1]
    l = lw["layer"]
    tm = main["tiles"]["mm"]
    att3, rw_pad = lw["att3"], lw["rw_pad"]
    both = (main, extra)

    mm_in = lambda w, layer, col0, n, tn: _matmul([main["h"]], w, layer, col0, n, tm, tn, w_is_t=True,
                                                  extra=[extra["h"]])
    p_qkv = mm_in(lw["w_in_t"], l, 0, att3, _pick_tile(att3, 512))
    p_rw = mm_in(lw["w_in_t"], l, att3, rw_pad, _pick_tile(math.gcd(att3, rw_pad), 512))
    n_cv = lw["w_cv_t"].shape[1]
    p_cv = mm_in(lw["w_cv_t"], 0, 0, n_cv, _pick_tile(n_cv, 512))

    ys, sts = [], []
    for gi, g in enumerate(both):
        y, st = _mixers(p_qkv[gi], p_rw[gi], p_cv[gi], g["nseq"], g["t_len"], lw, g["attn"], g["states"],
                        g["tiles"])
        ys.append(y)
        sts.append(st)

    mix = _matmul(ys[0], lw["w_o"], l, 0, d, tm, _pick_tile(d, 512), extra=ys[1])
    dff = lw["ffn_conv_w"].shape[1]
    x1s, h2s = [], []
    for gi, g in enumerate(both):
        (_, _, gt_m, sh_f, sc_f, _) = g["mods"]
        t = g["tiles"]
        x1, h2 = _resid(mix[gi], g["x"], lw["g_post_mix"], gt_m, t["ew"], t["tiles_per_mod"],
                        nxt=(lw["g_pre_ffn"], sc_f, sh_f))
        x1s.append(x1)
        h2s.append(h2)
    f0, ffn1_0, f1, ffn1_1 = _ffn_in(h2s[0], lw["w_ffn_in"], l, main["states"][3], lw["ffn_conv_w"],
                                     main["tiles"]["ffn"], _pick_tile(dff, 256), main["t_len"],
                                     extra=(h2s[1], extra["states"][3], extra["t_len"]))
    fs = [f0, f1]
    sts = [sts[0] + (ffn1_0,), sts[1] + (ffn1_1,)]
    fo = _matmul([fs[0]], lw["w_ffn_out"], l, 0, d, main["tiles"]["mm_wide_k"], _pick_tile(d, 512),
                 extra=[fs[1]])
    return [(x1s[gi], fo[gi], both[gi]["mods"][5], sts[gi]) for gi in range(2)]


def kernel(x_prompt, x_sample, c_prompt, c_sample, cache_k, cache_v, state_wkv, state_shift, state_conv, state_ffn, page_table, w_ada, b_ada, g_pre_mix, g_post_mix, g_pre_ffn, g_post_ffn, w_in, rw_mu, rw_w0, rw_w_up, rw_a0, rw_a_up, rw_g_up, rw_k_k, rw_k_a, rw_r_k, rw_ln_w, rw_ln_b, conv_w, w_o, w_ffn_in, ffn_conv_w, w_ffn_out):
    nb_p, s_len, d = x_prompt.shape
    nb_s, t_s, _ = x_sample.shape
    depth = w_in.shape[0]
    att = cache_k.shape[2] * HEAD_DIM
    n_heads = att // HEAD_DIM
    rdim = rw_w0.shape[1]
    rw_proj = rw_mu.shape[1]
    lora_ab = rw_w_up.shape[1] + rw_a_up.shape[1]
    lora_g = rw_g_up.shape[1]
    lora_g_pad = _round_up(lora_g, LANES)
    rw_pad = _round_up(3 * rdim + lora_ab + lora_g_pad, 512)
    lora_g_pad = rw_pad - 3 * rdim - lora_ab
    assert lora_ab == LANES
    past_len = page_table.shape[1] * PAGE_SIZE
    assert past_len % MOBA_BLOCK == 0 and s_len % MOBA_BLOCK == 0
    n_full = past_len // MOBA_BLOCK
    conv_dim = conv_w.shape[2]
    dff = ffn_conv_w.shape[2]

    xp = x_prompt.reshape(nb_p * s_len, d)
    xs = x_sample.reshape(nb_s * t_s, d)
    n_c = nb_p + nb_s
    c_all = jnp.concatenate([c_prompt, c_sample, jnp.zeros((_round_up(n_c, SUBLANES) - n_c, d), F32)], 0)

    cos_p, sin_p = _rope_tables(jnp.arange(s_len, dtype=jnp.int32))
    cos_s, sin_s = _rope_tables(past_len + jnp.arange(t_s, dtype=jnp.int32))

    tiles_p = dict(mm=_pick_tile(nb_p * s_len, 1024), mm_wide_k=_pick_tile(nb_p * s_len, 512),
                   ffn=_pick_tile(s_len, 1024), ew=MOBA_BLOCK, seq=MOBA_BLOCK, scan=MOBA_BLOCK // 4,
                   tiles_per_mod=s_len // MOBA_BLOCK)
    tiles_s = dict(mm=nb_s * t_s, mm_wide_k=nb_s * t_s, ffn=nb_s * t_s, ew=nb_s * t_s, seq=t_s, scan=t_s,
                   tiles_per_mod=1)

    zeros_p = (jnp.zeros((nb_p, 1, rw_pad), F32),
               jnp.zeros((nb_p, rdim // RWKV_HEAD_DIM, RWKV_HEAD_DIM, RWKV_HEAD_DIM), F32),
               jnp.zeros((nb_p, 2, conv_dim), F32), jnp.zeros((nb_p, 2, dff), F32))

    w_in_t = jnp.swapaxes(w_in, 1, 2)
    w_ffn_out_bf = w_ffn_out.astype(BF16)

    def layer_weights(l):
        pad_rows = lambda w, before, total: jnp.pad(w, ((before, total - before - w.shape[0]), (0, 0)))
        return dict(
            layer=l, att3=3 * att, rw_pad=rw_pad,
            w_in_t=w_in_t, w_cv_t=w_in_t[l, 3 * att + rw_proj:, :][None],
            w_o=w_o, w_ffn_in=w_ffn_in, w_ffn_out=w_ffn_out_bf,
            mu=jnp.pad(rw_mu[l], (0, rw_pad - rw_proj)), w0=rw_w0[l], a0=rw_a0[l],
            wup=pad_rows(rw_w_up[l], 0, lora_ab).astype(BF16),
            aup=pad_rows(rw_a_up[l], rw_w_up.shape[1], lora_ab).astype(BF16),
            gup=pad_rows(rw_g_up[l], 0, lora_g_pad).astype(BF16),
            k_k=rw_k_k[l], k_a=rw_k_a[l], r_k=rw_r_k[l], ln_w=rw_ln_w[l], ln_b=rw_ln_b[l],
            conv_w=conv_w[l], ffn_conv_w=ffn_conv_w[l],
            g_post_mix=g_post_mix[l], g_pre_ffn=g_pre_ffn[l],
            rdim=rdim, rw_proj=rw_proj, lora_ab=lora_ab)

    def attn_prompt(p_qkv):
        return _moba_prompt(p_qkv, nb_p, s_len, n_heads, cos_p, sin_p)

    def make_attn_sample(l):
        def fn(p_qkv):
            q_s, k_s, v_s = _rope_sample(p_qkv, nb_s, t_s, n_heads, cos_s, sin_s)
            means = _block_means(cache_k, l, page_table, n_full)
            idx = _topk_sample(q_s, means, n_full)
            idx_flat = idx[..., :MOBA_TOPK].reshape(-1)
            y = _attn_sample(q_s, k_s, v_s, cache_k, cache_v, l, idx_flat, page_table)
            return y.astype(BF16), k_s, v_s
        return fn

    outs_p, outs_s = [], []
    hp = hs = None
    fo_p = fo_s = gtf_p = gtf_s = None
    g_post_prev = None
    for l in range(depth):
        ada = _ada(c_all, w_ada, l, b_ada[l])
        mods_p = _mods(ada[:nb_p], d, False, s_len)
        mods_s = _mods(ada[nb_p:n_c], d, True, t_s)
        if l == 0:
            hp = _norm_mod(xp, g_pre_mix[l], mods_p[1], mods_p[0], tiles_p["ew"], tiles_p["tiles_per_mod"])
            hs = _norm_mod(xs, g_pre_mix[l], mods_s[1], mods_s[0], tiles_s["ew"], 1)
        else:
            xp, hp = _resid(fo_p, xp, g_post_prev, gtf_p, tiles_p["ew"], tiles_p["tiles_per_mod"],
                            nxt=(g_pre_mix[l], mods_p[1], mods_p[0]))
            xs, hs = _resid(fo_s, xs, g_post_prev, gtf_s, tiles_s["ew"], 1,
                            nxt=(g_pre_mix[l], mods_s[1], mods_s[0]))
        states_s = (jnp.pad(state_shift[l], ((0, 0), (0, rw_pad - rw_proj)))[:, None, :], state_wkv[l],
                    state_conv[l], state_ffn[l])
        main = dict(x=xp, h=hp, mods=mods_p, nseq=nb_p, t_len=s_len, attn=attn_prompt, states=zeros_p,
                    tiles=tiles_p)
        extra = dict(x=xs, h=hs, mods=mods_s, nseq=nb_s, t_len=t_s, attn=make_attn_sample(l),
                     states=states_s, tiles=tiles_s)
        (xp, fo_p, gtf_p, st_p), (xs, fo_s, gtf_s, st_s) = _layer(main, extra, layer_weights(l))
        g_post_prev = g_post_ffn[l]
        outs_p.append(st_p)
        outs_s.append(st_s)
    xp, _ = _resid(fo_p, xp, g_post_prev, gtf_p, tiles_p["ew"], tiles_p["tiles_per_mod"])
    xs, _ = _resid(fo_s, xs, g_post_prev, gtf_s, tiles_s["ew"], 1)

    stack = lambda sts, i: jnp.stack([st[i] for st in sts])
    return (xp.reshape(nb_p, s_len, d), xs.reshape(nb_s, t_s, d),
            stack(outs_p, 0), stack(outs_p, 1), stack(outs_s, 0), stack(outs_s, 1),
            stack(outs_p, 2), stack(outs_s, 2), stack(outs_p, 3), stack(outs_s, 3),
            stack(outs_p, 4), stack(outs_s, 4), stack(outs_p, 5), stack(outs_s, 5))
```
